```python
import jax, jax.numpy as jnp
from jax import lax
import numpy as np

D_MODEL = 1024
BATCH = 32
SEQ = 2048
DEPTH = 1

D_MIX = D_MODEL
HEAD_DIM = 64
RWKV_WIDTH = D_MIX // 2
RWKV_HEADS = RWKV_WIDTH // HEAD_DIM
ATTN_WIDTH = D_MIX - RWKV_WIDTH
ATTN_Q_HEADS = ATTN_WIDTH // HEAD_DIM
ATTN_KV_HEADS = 2
ATTN_GROUP = ATTN_Q_HEADS // ATTN_KV_HEADS
ATTN_KV_WIDTH = ATTN_KV_HEADS * HEAD_DIM
WINDOW = 128
D_DECAY_LORA = 32
D_AAA_LORA = 32
D_GATE_LORA = 96
RWKV_COLS = 3 * RWKV_WIDTH + D_DECAY_LORA + D_AAA_LORA + D_GATE_LORA
ATTN_COLS = ATTN_WIDTH + 2 * ATTN_KV_WIDTH
IN_COLS = RWKV_COLS + ATTN_COLS
PEER_HEADS = 8
PEER_N_KEYS = 128
PEER_N_EXPERTS = PEER_N_KEYS * PEER_N_KEYS
PEER_D_KEY = 256
PEER_HALF = PEER_D_KEY // 2
PEER_TOPK = 16
PEER_TOKEN_BLOCK = 128
RMS_EPS = 1e-6
LNX_EPS = 64e-5
NEG_INF = -1e30

kernel_name = "hymba_rwkv7_swa_sink_peer_block"


def rms_norm(x, g):
    xf = x.astype(jnp.float32)
    y = xf * lax.rsqrt(jnp.mean(xf * xf, axis=-1, keepdims=True) + RMS_EPS)
    return (y * g.astype(jnp.float32)).astype(x.dtype)


def token_shift(p, mu):
    prev = jnp.pad(p[:, :-1], ((0, 0), (1, 0), (0, 0)))
    return p + mu * (prev - p)


def rwkv7_mixer(p, mu, w0, w_up, a0, a_up, g_up, k_k, k_a, r_k, lnx_w, lnx_b):
    B, S, _ = p.shape
    H, Dh, W = RWKV_HEADS, HEAD_DIM, RWKV_WIDTH
    f32 = jnp.float32
    p = token_shift(p.astype(f32), mu.astype(f32))
    r = p[..., :W]
    k = p[..., W:2 * W]
    v = p[..., 2 * W:3 * W]
    xw = p[..., 3 * W:3 * W + D_DECAY_LORA]
    xa = p[..., 3 * W + D_DECAY_LORA:3 * W + D_DECAY_LORA + D_AAA_LORA]
    xg = p[..., 3 * W + D_DECAY_LORA + D_AAA_LORA:]
    w = -jax.nn.softplus(-(w0 + jnp.tanh(xw) @ w_up)) - 0.5
    decay = jnp.exp(-jnp.exp(w))
    a = jax.nn.sigmoid(a0 + xa @ a_up)
    g = jax.nn.sigmoid(xg) @ g_up
    heads = lambda t: t.reshape(B, S, H, Dh)
    kk = heads(k * k_k)
    kk = kk / jnp.maximum(jnp.sqrt(jnp.sum(kk * kk, axis=-1, keepdims=True)), 1e-12)
    k = k * (1.0 + (a - 1.0) * k_a)
    r_h, k_h, v_h, a_h, w_h = heads(r), heads(k), heads(v), heads(a), heads(decay)
    tm = lambda t: jnp.moveaxis(t, 1, 0)
    seqs = (tm(r_h), tm(w_h), tm(k_h), tm(v_h), tm(-kk), tm(kk * a_h))

    def step(state, inp):
        r_t, w_t, k_t, v_t, a_t, b_t = inp
        sa = jnp.einsum('bhvk,bhk->bhv', state, a_t)
        state = (state * w_t[:, :, None, :] + sa[..., None] * b_t[:, :, None, :]
                 + v_t[..., None] * k_t[:, :, None, :])
        return state, jnp.einsum('bhvk,bhk->bhv', state, r_t)

    s0 = jnp.zeros((B, H, Dh, Dh), f32)
    _, ys = lax.scan(step, s0, seqs)
    y = jnp.moveaxis(ys, 0, 1)
    mean = jnp.mean(y, axis=-1, keepdims=True)
    var = jnp.mean(jnp.square(y - mean), axis=-1, keepdims=True)
    y = ((y - mean) * lax.rsqrt(var + LNX_EPS)).reshape(B, S, W) * lnx_w + lnx_b
    bonus = jnp.sum(r_h * k_h * r_k, axis=-1, keepdims=True) * v_h
    y = (y + bonus.reshape(B, S, W)) * g
    return y


def swa_sink_attention(q, k, v, sinks):
    B, S, _ = q.shape
    nb = S // WINDOW
    KV, G, Dh = ATTN_KV_HEADS, ATTN_GROUP, HEAD_DIM
    f32 = jnp.float32
    scale = 1.0 / np.sqrt(Dh)
    q = q.reshape(B, nb, WINDOW, KV, G, Dh)
    k = k.reshape(B, nb, WINDOW, KV, Dh)
    v = v.reshape(B, nb, WINDOW, KV, Dh)
    prev = lambda t: jnp.concatenate([jnp.zeros_like(t[:, :1]), t[:, :-1]], axis=1)
    k_ext = jnp.concatenate([prev(k), k], axis=2)
    v_ext = jnp.concatenate([prev(v), v], axis=2)
    qi = jnp.arange(WINDOW)[:, None]
    kj = jnp.arange(2 * WINDOW)[None, :]
    diff = qi + WINDOW - kj
    band = (diff >= 0) & (diff < WINDOW)
    in_current = kj >= WINDOW
    sink_logits = sinks.astype(f32).reshape(KV, G, 1, 1)

    def block(args):
        q_b, k_b, v_b, n = args
        allowed = band & ((n > 0) | in_current)
        s = jnp.einsum('bqkgd,bskd->bkgqs', q_b.astype(f32), k_b.astype(f32)) * scale
        s = jnp.where(allowed, s, NEG_INF)
        sink = jnp.broadcast_to(sink_logits, s.shape[:-1] + (1,))
        pr = jax.nn.softmax(jnp.concatenate([s, sink], axis=-1), axis=-1)[..., :-1]
        o = jnp.einsum('bkgqs,bskd->bqkgd', pr, v_b.astype(f32))
        return o.astype(q_b.dtype)

    mv = lambda t: jnp.moveaxis(t, 1, 0)
    o = lax.map(block, (mv(q), mv(k_ext), mv(v_ext), jnp.arange(nb)))
    return jnp.moveaxis(o, 0, 1).reshape(B, S, ATTN_WIDTH)


def peer_ffn(h, wq, subkeys, u_tab, v_tab):
    B, S, D = h.shape
    T = B * S
    nblk = T // PEER_TOKEN_BLOCK
    K, H = PEER_TOPK, PEER_HEADS
    f32 = jnp.float32
    xb = h.reshape(nblk, PEER_TOKEN_BLOCK, D)

    def block(x_b):
        q = (x_b @ wq).reshape(-1, H, 2, PEER_HALF).astype(f32)
        s = jnp.einsum('thcd,hcnd->thcn', q, subkeys.astype(f32))
        s_top, i_top = lax.top_k(s, K)
        cand_s = (s_top[:, :, 0, :, None] + s_top[:, :, 1, None, :]).reshape(-1, H, K * K)
        cand_i = (i_top[:, :, 0, :, None] * PEER_N_KEYS + i_top[:, :, 1, None, :]).reshape(-1, H, K * K)
        best_s, best_pos = lax.top_k(cand_s, K)
        idx = jnp.take_along_axis(cand_i, best_pos, axis=-1).reshape(-1, H * K)
        gate = jax.nn.softmax(best_s, axis=-1).reshape(-1, H * K).astype(x_b.dtype)
        u = u_tab[idx]
        act = jax.nn.gelu(jnp.einsum('td,ted->te', x_b, u), approximate=False)
        return jnp.einsum('te,ted->td', gate * act, v_tab[idx])

    y = lax.map(block, xb)
    return y.reshape(B, S, D)


def setup_inputs(seed: int = 0) -> dict:
    key = jax.random.key(seed)
    ks = jax.random.split(key, 26)
    L, D, W = DEPTH, D_MODEL, RWKV_WIDTH
    nrm = lambda k, shape, s: jax.random.normal(k, shape, jnp.float32) * s
    return {
        "x": nrm(ks[0], (BATCH, SEQ, D), 1.0),
        "ln1_g": 1.0 + nrm(ks[1], (L, D), 0.02),
        "w_in": nrm(ks[2], (L, D, IN_COLS), D ** -0.5),
        "b_attn": nrm(ks[3], (L, ATTN_COLS), 0.02),
        "mu_shift": jax.random.uniform(ks[4], (L, RWKV_COLS), jnp.float32),
        "w0": jax.random.uniform(ks[5], (L, W), jnp.float32, -4.0, 1.0),
        "w_up": nrm(ks[6], (L, D_DECAY_LORA, W), 0.1),
        "a0": nrm(ks[7], (L, W), 0.1),
        "a_up": nrm(ks[8], (L, D_AAA_LORA, W), 0.1),
        "g_up": nrm(ks[9], (L, D_GATE_LORA, W), D_GATE_LORA ** -0.5),
        "k_k": 0.85 + nrm(ks[10], (L, W), 0.02),
        "k_a": 1.0 + nrm(ks[11], (L, W), 0.02),
        "r_k": nrm(ks[12], (L, RWKV_HEADS, HEAD_DIM), 0.1),
        "lnx_w": 1.0 + nrm(ks[13], (L, W), 0.02),
        "lnx_b": nrm(ks[14], (L, W), 0.02),
        "attn_sinks": nrm(ks[15], (L, ATTN_Q_HEADS), 1.0),
        "attn_norm_g": 1.0 + nrm(ks[16], (L, ATTN_WIDTH), 0.02),
        "w_out": nrm(ks[17], (L, D_MIX, D), D_MIX ** -0.5),
        "ln2_g": 1.0 + nrm(ks[18], (L, D), 0.02),
        "peer_wq": nrm(ks[19], (L, D, PEER_HEADS * PEER_D_KEY), D ** -0.5),
        "peer_subkeys": nrm(ks[20], (L, PEER_HEADS, 2, PEER_N_KEYS, PEER_HALF), PEER_HALF ** -0.5),
        "peer_u": nrm(ks[21], (L, PEER_N_EXPERTS, D), D ** -0.5),
        "peer_v": nrm(ks[22], (L, PEER_N_EXPERTS, D), PEER_HEADS ** -0.5),
        "lnf_g": 1.0 + nrm(ks[23], (D,), 0.02),
    }


def reference(x, ln1_g, w_in, b_attn, mu_shift, w0, w_up, a0, a_up, g_up, k_k, k_a, r_k,
              lnx_w, lnx_b, attn_sinks, attn_norm_g, w_out, ln2_g, peer_wq, peer_subkeys,
              peer_u, peer_v, lnf_g):
    for l in range(DEPTH):
        h = rms_norm(x, ln1_g[l])
        p = h @ w_in[l]
        p_rwkv = p[..., :RWKV_COLS]
        p_attn = p[..., RWKV_COLS:] + b_attn[l]
        y_rwkv = rwkv7_mixer(p_rwkv, mu_shift[l], w0[l], w_up[l], a0[l], a_up[l], g_up[l],
                             k_k[l], k_a[l], r_k[l], lnx_w[l], lnx_b[l]).astype(x.dtype)
        q = p_attn[..., :ATTN_WIDTH]
        k = p_attn[..., ATTN_WIDTH:ATTN_WIDTH + ATTN_KV_WIDTH]
        v = p_attn[..., ATTN_WIDTH + ATTN_KV_WIDTH:]
        y_attn = rms_norm(swa_sink_attention(q, k, v, attn_sinks[l]), attn_norm_g[l])
        x = x + jnp.concatenate([y_rwkv, y_attn], axis=-1) @ w_out[l]
        x = x + peer_ffn(rms_norm(x, ln2_g[l]), peer_wq[l], peer_subkeys[l], peer_u[l], peer_v[l])
    return rms_norm(x, lnf_g)
```

```python
import functools

import jax
import jax.numpy as jnp
import numpy as np
from jax import lax
from jax.experimental import pallas as pl
from jax.experimental.pallas import tpu as pltpu

D_MODEL = 1024
HEAD_DIM = 64
RWKV_WIDTH = 512
RWKV_HEADS = 8
ATTN_WIDTH = 512
ATTN_Q_HEADS = 8
ATTN_KV_HEADS = 2
ATTN_GROUP = 4
ATTN_KV_WIDTH = 128
WINDOW = 128
D_DECAY_LORA = 32
D_AAA_LORA = 32
D_GATE_LORA = 96
LORA_COLS = D_DECAY_LORA + D_AAA_LORA + D_GATE_LORA
LORA_PAD = 256
RKV_COLS = 3 * RWKV_WIDTH
ATTN_COLS = ATTN_WIDTH + 2 * ATTN_KV_WIDTH
PEER_HEADS = 8
PEER_N_KEYS = 128
PEER_HALF = 128
PEER_TOPK = 16
RMS_EPS = 1e-6
LNX_EPS = 64e-5
NEG_INF = -1e30

LANES = 128
SUBLANES = 8
VMEM_LIMIT_BYTES = 56 * 1024 * 1024

RWKV_CHUNK = 64
HEAD_PAIR = 2 * HEAD_DIM

_HI = lax.Precision.HIGHEST
_NN = (((1,), (0,)), ((), ()))
_NT = (((1,), (1,)), ((), ()))
_TN = (((0,), (0,)), ((), ()))


def _split(a):
    hi = a.astype(jnp.bfloat16)
    lo = (a - hi.astype(jnp.float32)).astype(jnp.bfloat16)
    return hi, lo


def _dot32(a, b, dims=_NN):
    ah, al = _split(a)
    bh, bl = _split(b)
    d = lambda p, q: lax.dot_general(p, q, dims, preferred_element_type=jnp.float32)
    return d(ah, bh) + (d(ah, bl) + d(al, bh))


def _dot16(a, b, dims=_NN):
    return lax.dot_general(a.astype(jnp.bfloat16), b.astype(jnp.bfloat16), dims,
                           preferred_element_type=jnp.float32)


def _cparams(sem):
    return pltpu.CompilerParams(dimension_semantics=sem, vmem_limit_bytes=VMEM_LIMIT_BYTES)


def _inproj_kernel(x_ref, g_ref, w_ref, b_ref, rkv_ref, lora_ref, attn_ref):
    x = x_ref[...]
    ms = jnp.mean(x * x, axis=-1, keepdims=True)
    h = (x * lax.rsqrt(ms + RMS_EPS) * g_ref[...]).astype(jnp.bfloat16)
    p = jnp.dot(h, w_ref[...], preferred_element_type=jnp.float32)
    rkv_ref[...] = p[:, :RKV_COLS]
    lora_ref[...] = p[:, RKV_COLS:RKV_COLS + LORA_PAD]
    attn_ref[...] = p[:, RKV_COLS + LORA_PAD:] + b_ref[...]


def _inproj(x2, ln1_g, w_in, b_attn, tm):
    T = x2.shape[0]
    w_rkv = w_in[:, :RKV_COLS]
    w_lora = jnp.pad(w_in[:, RKV_COLS:RKV_COLS + LORA_COLS], ((0, 0), (0, LORA_PAD - LORA_COLS)))
    w_attn = w_in[:, RKV_COLS + LORA_COLS:]
    w_all = jnp.concatenate([w_rkv, w_lora, w_attn], axis=1).astype(jnp.bfloat16)
    ncols = w_all.shape[1]
    return pl.pallas_call(
        _inproj_kernel,
        grid=(T // tm,),
        in_specs=[
            pl.BlockSpec((tm, D_MODEL), lambda i: (i, 0)),
            pl.BlockSpec((1, D_MODEL), lambda i: (0, 0)),
            pl.BlockSpec((D_MODEL, ncols), lambda i: (0, 0)),
            pl.BlockSpec((1, ATTN_COLS), lambda i: (0, 0)),
        ],
        out_specs=[
            pl.BlockSpec((tm, RKV_COLS), lambda i: (i, 0)),
            pl.BlockSpec((tm, LORA_PAD), lambda i: (i, 0)),
            pl.BlockSpec((tm, ATTN_COLS), lambda i: (i, 0)),
        ],
        out_shape=[
            jax.ShapeDtypeStruct((T, RKV_COLS), jnp.float32),
            jax.ShapeDtypeStruct((T, LORA_PAD), jnp.float32),
            jax.ShapeDtypeStruct((T, ATTN_COLS), jnp.float32),
        ],
        compiler_params=_cparams(("parallel",)),
        name="inproj",
    )(x2, ln1_g.reshape(1, D_MODEL), w_all, b_attn.reshape(1, ATTN_COLS))


def _rwkv_kernel(r_ref, k_ref, v_ref, lora_ref, mur_ref, muk_ref, muv_ref, mul_ref,
                 w0_ref, wup_ref, a0_ref, aup_ref, gup_ref, kk_ref, ka_ref, rk_ref,
                 lnw_ref, lnb_ref, o_ref, s_scr, rkv_buf, lora_buf):
    C = RWKV_CHUNK
    P = HEAD_PAIR
    f32 = jnp.float32
    c = pl.program_id(2)

    @pl.when(c == 0)
    def _():
        s_scr[...] = jnp.zeros_like(s_scr)
        rkv_buf[...] = jnp.zeros_like(rkv_buf)
        lora_buf[...] = jnp.zeros_like(lora_buf)

    def shifted(buf, j, cur, mu):
        buf[j, SUBLANES:SUBLANES + C, :] = cur
        prev = buf[j, SUBLANES - 1:SUBLANES - 1 + C, :]
        buf[j, SUBLANES - 1:SUBLANES, :] = cur[C - 1:C, :]
        return cur + mu * (prev - cur)

    r = shifted(rkv_buf, 0, r_ref[0], mur_ref[...])
    k = shifted(rkv_buf, 1, k_ref[0], muk_ref[...])
    v = shifted(rkv_buf, 2, v_ref[0], muv_ref[...])
    xl = shifted(lora_buf, 0, lora_ref[0], mul_ref[...])

    w_raw = w0_ref[...] + _dot32(jnp.tanh(xl), wup_ref[...])
    z = -w_raw
    softplus = jnp.maximum(z, 0.0) + jnp.log(1.0 + jnp.exp(-jnp.abs(z)))
    lw = -jnp.exp(-softplus - 0.5)
    a = jax.nn.sigmoid(a0_ref[...] + _dot32(xl, aup_ref[...]))
    g = _dot32(jax.nn.sigmoid(xl), gup_ref[...])

    lane = lax.broadcasted_iota(jnp.int32, (1, P), 1)
    m0 = lane < HEAD_DIM
    rowp = lax.broadcasted_iota(jnp.int32, (P, P), 0)
    colp = lax.broadcasted_iota(jnp.int32, (P, P), 1)
    same_head = (rowp < HEAD_DIM) == (colp < HEAD_DIM)
    bd = jnp.where(same_head, 1.0, 0.0).astype(f32)

    kk = k * kk_ref[...]
    nrm = jnp.sqrt(_dot32(kk * kk, bd))
    kk = kk / jnp.maximum(nrm, 1e-12)
    k2 = k * (1.0 + (a - 1.0) * ka_ref[...])
    bonus = _dot32(r * k2 * rk_ref[...], bd) * v

    ti = lax.broadcasted_iota(jnp.int32, (C, C), 0)
    tj = lax.broadcasted_iota(jnp.int32, (C, C), 1)
    incl = ti >= tj
    strict = ti > tj
    lc = _dot32(jnp.where(incl, 1.0, 0.0).astype(f32), lw)
    e_pos = jnp.exp(lc)
    e_neg = jnp.exp(-lc)
    rt = r * e_pos
    at = -kk * jnp.exp(lc - lw)
    bt = kk * a * e_neg
    kt = k2 * e_neg
    w_end = e_pos[C - 1:C, :]

    zero = jnp.zeros_like(at)
    stack = lambda t: jnp.concatenate([jnp.where(m0, t, zero), jnp.where(m0, zero, t)], axis=0)
    twice = lambda t: jnp.concatenate([t, t], axis=0)
    a2, r2, b2, k2s = stack(at), stack(rt), stack(bt), stack(kt)
    strict2 = same_head & ((rowp & (C - 1)) > (colp & (C - 1)))
    incl2 = same_head & ((rowp & (C - 1)) >= (colp & (C - 1)))
    mab = jnp.where(strict2, _dot32(a2, b2, _NT), 0.0)
    mak = jnp.where(strict2, _dot32(a2, k2s, _NT), 0.0)
    nrb = jnp.where(incl2, _dot32(r2, b2, _NT), 0.0)
    nrk = jnp.where(incl2, _dot32(r2, k2s, _NT), 0.0)
    s0 = s_scr[...]
    v2 = twice(v)
    x = twice(_dot32(at, s0, _NT)) + _dot32(mak, v2)
    m = mab
    for step in range(6):
        x = x + _dot32(m, x)
        if step < 5:
            m = _dot32(m, m)
    y2 = twice(_dot32(rt, s0, _NT)) + _dot32(jnp.concatenate([nrb, nrk], axis=1),
                                             jnp.concatenate([x, v2], axis=0))
    u = jnp.where(m0, x[:C], x[C:])
    y = jnp.where(m0, y2[:C], y2[C:])

    ds = _dot32(jnp.concatenate([u, v], axis=0), jnp.concatenate([bt, kt], axis=0), _TN)
    s_scr[...] = jnp.where(same_head, s0 + ds, 0.0) * w_end

    mean = _dot32(y, bd) * (1.0 / HEAD_DIM)
    yc = y - mean
    var = _dot32(yc * yc, bd) * (1.0 / HEAD_DIM)
    yn = yc * lax.rsqrt(var + LNX_EPS) * lnw_ref[...] + lnb_ref[...]
    o_ref[0] = (yn + bonus) * g


def _rwkv(p_rkv, p_lora, mu_shift, w0, w_up, a0, a_up, g_up, k_k, k_a, r_k, lnx_w, lnx_b):
    B, S, _ = p_rkv.shape
    C = RWKV_CHUNK
    W = RWKV_WIDTH
    npair = W // HEAD_PAIR
    row = lambda t: t.reshape(1, -1)
    mu_r, mu_k, mu_v = (row(mu_shift[i * W:(i + 1) * W]) for i in range(3))
    mu_l = row(jnp.pad(mu_shift[3 * W:], (0, LORA_PAD - LORA_COLS)))
    wup_p = jnp.zeros((LORA_PAD, W), jnp.float32).at[:D_DECAY_LORA].set(w_up)
    aup_p = jnp.zeros((LORA_PAD, W), jnp.float32).at[D_DECAY_LORA:D_DECAY_LORA + D_AAA_LORA].set(a_up)
    gup_p = jnp.zeros((LORA_PAD, W), jnp.float32).at[D_DECAY_LORA + D_AAA_LORA:LORA_COLS].set(g_up)
    vec = lambda: pl.BlockSpec((1, HEAD_PAIR), lambda b, p, c: (0, p))
    mat = lambda: pl.BlockSpec((LORA_PAD, HEAD_PAIR), lambda b, p, c: (0, p))
    col = lambda j: pl.BlockSpec((1, C, HEAD_PAIR), lambda b, p, c, j=j: (b, c, j * npair + p))
    return pl.pallas_call(
        _rwkv_kernel,
        grid=(B, npair, S // C),
        in_specs=[
            col(0), col(1), col(2),
            pl.BlockSpec((1, C, LORA_PAD), lambda b, p, c: (b, c, 0)),
            vec(), vec(), vec(),
            pl.BlockSpec((1, LORA_PAD), lambda b, p, c: (0, 0)),
            vec(), mat(), vec(), mat(), mat(), vec(), vec(), vec(), vec(), vec(),
        ],
        out_specs=pl.BlockSpec((1, C, HEAD_PAIR), lambda b, p, c: (b, c, p)),
        out_shape=jax.ShapeDtypeStruct((B, S, W), jnp.float32),
        scratch_shapes=[
            pltpu.VMEM((HEAD_PAIR, HEAD_PAIR), jnp.float32),
            pltpu.VMEM((3, SUBLANES + C, HEAD_PAIR), jnp.float32),
            pltpu.VMEM((1, SUBLANES + C, LORA_PAD), jnp.float32),
        ],
        compiler_params=_cparams(("parallel", "parallel", "arbitrary")),
        name="rwkv7",
    )(p_rkv, p_rkv, p_rkv, p_lora, mu_r, mu_k, mu_v, mu_l,
      row(w0), wup_p, row(a0), aup_p, gup_p, row(k_k), row(k_a), row(r_k), row(lnx_w), row(lnx_b))


def _attn_kernel(sink_ref, q_ref, kp_ref, kc_ref, vp_ref, vc_ref, g_ref, o_ref):
    n = pl.program_id(1)
    Wn = WINDOW
    scale = 1.0 / np.sqrt(HEAD_DIM)
    q = q_ref[0]
    kext = jnp.concatenate([kp_ref[0], kc_ref[0]], axis=0)
    vext = jnp.concatenate([vp_ref[0], vc_ref[0]], axis=0)
    qi = lax.broadcasted_iota(jnp.int32, (Wn, 2 * Wn), 0)
    kj = lax.broadcasted_iota(jnp.int32, (Wn, 2 * Wn), 1)
    diff = qi + Wn - kj
    allowed = (diff >= 0) & (diff < Wn) & ((n > 0) | (kj >= Wn))
    outs = []
    for h in range(ATTN_Q_HEADS):
        kv = h // ATTN_GROUP
        qh = q[:, h * HEAD_DIM:(h + 1) * HEAD_DIM]
        kh = kext[:, kv * HEAD_DIM:(kv + 1) * HEAD_DIM]
        vh = vext[:, kv * HEAD_DIM:(kv + 1) * HEAD_DIM]
        s = _dot16(qh, kh, _NT) * scale
        s = jnp.where(allowed, s, NEG_INF)
        sink = sink_ref[h]
        m = jnp.maximum(jnp.max(s, axis=-1, keepdims=True), sink)
        p = jnp.exp(s - m)
        denom = jnp.sum(p, axis=-1, keepdims=True) + jnp.exp(sink - m)
        outs.append(_dot16(p, vh) / denom)
    o = jnp.concatenate(outs, axis=-1)
    ms = jnp.mean(o * o, axis=-1, keepdims=True)
    o_ref[0] = o * lax.rsqrt(ms + RMS_EPS) * g_ref[...]


def _attention(p_attn, sinks, norm_g):
    B, S, _ = p_attn.shape
    Wn = WINDOW
    nb = S // Wn
    kcol = ATTN_WIDTH // ATTN_KV_WIDTH
    prev = lambda b, n: (b, jnp.maximum(n - 1, 0))
    return pl.pallas_call(
        _attn_kernel,
        grid=(B, nb),
        in_specs=[
            pl.BlockSpec(memory_space=pltpu.SMEM),
            pl.BlockSpec((1, Wn, ATTN_WIDTH), lambda b, n: (b, n, 0)),
            pl.BlockSpec((1, Wn, ATTN_KV_WIDTH), lambda b, n: prev(b, n) + (kcol,)),
            pl.BlockSpec((1, Wn, ATTN_KV_WIDTH), lambda b, n: (b, n, kcol)),
            pl.BlockSpec((1, Wn, ATTN_KV_WIDTH), lambda b, n: prev(b, n) + (kcol + 1,)),
            pl.BlockSpec((1, Wn, ATTN_KV_WIDTH), lambda b, n: (b, n, kcol + 1)),
            pl.BlockSpec((1, ATTN_WIDTH), lambda b, n: (0, 0)),
        ],
        out_specs=pl.BlockSpec((1, Wn, ATTN_WIDTH), lambda b, n: (b, n, 0)),
        out_shape=jax.ShapeDtypeStruct((B, S, ATTN_WIDTH), jnp.float32),
        compiler_params=_cparams(("parallel", "parallel")),
        name="swa_attn",
    )(sinks, p_attn, p_attn, p_attn, p_attn, p_attn, norm_g.reshape(1, ATTN_WIDTH))


def _outproj_kernel(x_ref, yr_ref, ya_ref, wr_ref, wa_ref, g_ref, wq_ref, sk_ref, x1_ref, h2_ref, st_ref):
    x1 = (x_ref[...] + _dot16(yr_ref[...], wr_ref[...]) + _dot16(ya_ref[...], wa_ref[...]))
    x1_ref[...] = x1
    ms = jnp.mean(x1 * x1, axis=-1, keepdims=True)
    h2 = x1 * lax.rsqrt(ms + RMS_EPS) * g_ref[...]
    h2_ref[...] = h2
    q = _dot16(h2, wq_ref[...])
    for hc in range(2 * PEER_HEADS):
        st_ref[hc] = _dot16(sk_ref[hc], q[:, hc * PEER_HALF:(hc + 1) * PEER_HALF], _NT)


def _outproj(x2, y_rwkv, y_attn, w_out, ln2_g, peer_wq, peer_subkeys, tm):
    T = x2.shape[0]
    nq = peer_wq.shape[1]
    nhc = 2 * PEER_HEADS
    w_r = w_out[:RWKV_WIDTH].astype(jnp.bfloat16)
    w_a = w_out[RWKV_WIDTH:].astype(jnp.bfloat16)
    sk = peer_subkeys.reshape(nhc, PEER_N_KEYS, PEER_HALF).astype(jnp.bfloat16)
    full = lambda shape: pl.BlockSpec(shape, lambda i: (0,) * len(shape))
    return pl.pallas_call(
        _outproj_kernel,
        grid=(T // tm,),
        in_specs=[
            pl.BlockSpec((tm, D_MODEL), lambda i: (i, 0)),
            pl.BlockSpec((tm, RWKV_WIDTH), lambda i: (i, 0)),
            pl.BlockSpec((tm, ATTN_WIDTH), lambda i: (i, 0)),
            full((RWKV_WIDTH, D_MODEL)), full((ATTN_WIDTH, D_MODEL)), full((1, D_MODEL)),
            full((D_MODEL, nq)), full((nhc, PEER_N_KEYS, PEER_HALF)),
        ],
        out_specs=[
            pl.BlockSpec((tm, D_MODEL), lambda i: (i, 0)),
            pl.BlockSpec((tm, D_MODEL), lambda i: (i, 0)),
            pl.BlockSpec((nhc, PEER_N_KEYS, tm), lambda i: (0, 0, i)),
        ],
        out_shape=[
            jax.ShapeDtypeStruct((T, D_MODEL), jnp.float32),
            jax.ShapeDtypeStruct((T, D_MODEL), jnp.float32),
            jax.ShapeDtypeStruct((nhc, PEER_N_KEYS, T), jnp.float32),
        ],
        compiler_params=_cparams(("parallel",)),
        name="outproj_peerq",
    )(x2, y_rwkv, y_attn, w_r, w_a, ln2_g.reshape(1, D_MODEL), peer_wq.astype(jnp.bfloat16), sk)


def _top16(s, payload=None):
    n, w = s.shape
    K = PEER_TOPK
    row = lax.broadcasted_iota(jnp.int32, (n, w), 0)
    out_row = lax.broadcasted_iota(jnp.int32, (K, w), 0)

    def body(i, carry):
        cur, vals, idxs = carry
        m = jnp.max(cur, axis=0, keepdims=True)
        am = jnp.min(jnp.where(cur == m, row, n), axis=0, keepdims=True)
        hit = row == am
        pick = am if payload is None else jnp.max(jnp.where(hit, payload, -1), axis=0, keepdims=True)
        vals = jnp.where(out_row == i, m, vals)
        idxs = jnp.where(out_row == i, pick, idxs)
        return jnp.where(hit, -jnp.inf, cur), vals, idxs

    init = (s, jnp.zeros((K, w), jnp.float32), jnp.zeros((K, w), jnp.int32))
    _, vals, idxs = lax.fori_loop(0, K, body, init, unroll=True)
    return vals, idxs


def _topk_kernel(st_ref, idx_ref, gate_ref):
    K = PEER_TOPK
    ncol = st_ref.shape[2] // LANES

    def per_col(j, _):
        col = pl.ds(pl.multiple_of(j * LANES, LANES), LANES)
        for h in range(PEER_HEADS):
            s1, i1 = _top16(st_ref[2 * h, :, col])
            s2, i2 = _top16(st_ref[2 * h + 1, :, col])
            cand_s = jnp.concatenate([s1[a:a + 1] + s2 for a in range(K)], axis=0)
            cand_i = jnp.concatenate([i1[a:a + 1] * PEER_N_KEYS + i2 for a in range(K)], axis=0)
            best_s, best_i = _top16(cand_s, cand_i)
            e = jnp.exp(best_s - best_s[0:1])
            gate = e / jnp.sum(e, axis=0, keepdims=True)
            idx_ref[h * K:(h + 1) * K, col] = best_i
            gate_ref[h * K:(h + 1) * K, col] = gate
        return 0

    lax.fori_loop(0, ncol, per_col, 0)


def _topk(scores_t, tm):
    nhc, nk, T = scores_t.shape
    ne = PEER_HEADS * PEER_TOPK
    return pl.pallas_call(
        _topk_kernel,
        grid=(T // tm,),
        in_specs=[pl.BlockSpec((nhc, nk, tm), lambda i: (0, 0, i))],
        out_specs=[pl.BlockSpec((ne, tm), lambda i: (0, i)), pl.BlockSpec((ne, tm), lambda i: (0, i))],
        out_shape=[jax.ShapeDtypeStruct((ne, T), jnp.int32), jax.ShapeDtypeStruct((ne, T), jnp.float32)],
        compiler_params=_cparams(("parallel",)),
        name="peer_topk",
    )(scores_t)


PEER_TOKEN_TILE = 128
PEER_SELECTED = PEER_HEADS * PEER_TOPK
HALF_ROWS = D_MODEL // 2 // LANES
_HI_MASK = np.uint32(0xFFFF0000)


def _pack_table(tab):
    n, d = tab.shape
    b = lax.bitcast_convert_type(tab.astype(jnp.bfloat16), jnp.uint16).astype(jnp.uint32)
    w = (b[:, :d // 2] << 16) | b[:, d // 2:]
    return w.reshape(n, d // 2 // LANES, LANES)


def _unpack(w):
    hi = lax.bitcast_convert_type(w & _HI_MASK, jnp.float32)
    lo = lax.bitcast_convert_type(w << 16, jnp.float32)
    return hi, lo


def _peer_act_kernel(idx_ref, x_ref, gate_ref, tab_ref, c_ref, prod_scr, acc_scr):
    TB = PEER_TOKEN_TILE
    NE = PEER_SELECTED
    lane = lax.broadcasted_iota(jnp.int32, (NE, TB), 1)
    acc_scr[...] = jnp.zeros_like(acc_scr)

    def per_token(t, _):
        xt = x_ref[t]
        xh = xt[:HALF_ROWS]
        xl = xt[HALF_ROWS:]
        for e in range(NE):
            hi, lo = _unpack(tab_ref[idx_ref[e, t]])
            prod_scr[e * HALF_ROWS:(e + 1) * HALF_ROWS, :] = hi * xh + lo * xl
        part = prod_scr[pl.ds(0, NE, stride=HALF_ROWS), :]
        for s in range(1, HALF_ROWS):
            part = part + prod_scr[pl.ds(s, NE, stride=HALF_ROWS), :]
        act = jnp.sum(part, axis=-1, keepdims=True)
        acc_scr[...] = jnp.where(lane == t, act, acc_scr[...])
        return 0

    lax.fori_loop(0, TB, per_token, 0)
    a = acc_scr[...]
    gelu = 0.5 * a * (1.0 + lax.erf(a * np.float32(1.0 / np.sqrt(2.0))))
    c_ref[...] = gate_ref[...] * gelu


def _peer_act(idx_t, h2_3d, gate_t, u_packed):
    NE, T = idx_t.shape
    TB = PEER_TOKEN_TILE
    nrow = D_MODEL // LANES
    return pl.pallas_call(
        _peer_act_kernel,
        grid=(T // TB,),
        in_specs=[
            pl.BlockSpec((NE, TB), lambda i: (0, i), memory_space=pltpu.SMEM),
            pl.BlockSpec((TB, nrow, LANES), lambda i: (i, 0, 0)),
            pl.BlockSpec((NE, TB), lambda i: (0, i)),
            pl.BlockSpec(u_packed.shape, lambda i: (0, 0, 0), pipeline_mode=pl.Buffered(1)),
        ],
        out_specs=pl.BlockSpec((NE, TB), lambda i: (0, i)),
        out_shape=jax.ShapeDtypeStruct((NE, T), jnp.float32),
        scratch_shapes=[pltpu.VMEM((NE * HALF_ROWS, LANES), jnp.float32), pltpu.VMEM((NE, TB), jnp.float32)],
        compiler_params=_cparams(("arbitrary",)),
        name="peer_act",
    )(idx_t, h2_3d, gate_t, u_packed)


def _peer_out_kernel(idx_ref, c_ref, x1_ref, g_ref, tab_ref, o_ref):
    TB = PEER_TOKEN_TILE
    NE = PEER_SELECTED
    nacc = 4

    def per_token(t, _):
        accs = [None] * (2 * nacc)
        for e in range(NE):
            hi, lo = _unpack(tab_ref[idx_ref[e, t]])
            cw = c_ref[e, t]
            j = 2 * (e % nacc)
            accs[j] = cw * hi if accs[j] is None else accs[j] + cw * hi
            accs[j + 1] = cw * lo if accs[j + 1] is None else accs[j + 1] + cw * lo
        y_hi = (accs[0] + accs[2]) + (accs[4] + accs[6])
        y_lo = (accs[1] + accs[3]) + (accs[5] + accs[7])
        z = x1_ref[t] + jnp.concatenate([y_hi, y_lo], axis=0)
        ms = jnp.sum(jnp.sum(z * z, axis=1, keepdims=True), axis=0, keepdims=True) * (1.0 / D_MODEL)
        o_ref[t] = z * lax.rsqrt(ms + RMS_EPS) * g_ref[...]
        return 0

    lax.fori_loop(0, TB, per_token, 0)


def _peer_out(idx_t, c_t, x1_3d, lnf_g, v_packed):
    NE, T = idx_t.shape
    TB = PEER_TOKEN_TILE
    nrow = D_MODEL // LANES
    smem = lambda: pl.BlockSpec((NE, TB), lambda i: (0, i), memory_space=pltpu.SMEM)
    return pl.pallas_call(
        _peer_out_kernel,
        grid=(T // TB,),
        in_specs=[
            smem(), smem(),
            pl.BlockSpec((TB, nrow, LANES), lambda i: (i, 0, 0)),
            pl.BlockSpec((nrow, LANES), lambda i: (0, 0)),
            pl.BlockSpec(v_packed.shape, lambda i: (0, 0, 0), pipeline_mode=pl.Buffered(1)),
        ],
        out_specs=pl.BlockSpec((TB, nrow, LANES), lambda i: (i, 0, 0)),
        out_shape=jax.ShapeDtypeStruct((T, nrow, LANES), jnp.float32),
        compiler_params=_cparams(("arbitrary",)),
        name="peer_out",
    )(idx_t, c_t, x1_3d, lnf_g.reshape(nrow, LANES), v_packed)


def kernel(x, ln1_g, w_in, b_attn, mu_shift, w0, w_up, a0, a_up, g_up, k_k, k_a, r_k, lnx_w, lnx_b, attn_sinks, attn_norm_g, w_out, ln2_g, peer_wq, peer_subkeys, peer_u, peer_v, lnf_g):
    B, S, D = x.shape
    T = B * S
    x2 = x.reshape(T, D)
    p_rkv, p_lora, p_attn = _inproj(x2, ln1_g[0], w_in[0], b_attn[0], tm=512)
    y_rwkv = _rwkv(p_rkv.reshape(B, S, -1), p_lora.reshape(B, S, -1), mu_shift[0], w0[0], w_up[0], a0[0],
                   a_up[0], g_up[0], k_k[0], k_a[0], r_k[0].reshape(-1), lnx_w[0], lnx_b[0])
    y_attn = _attention(p_attn.reshape(B, S, -1), attn_sinks[0], attn_norm_g[0])
    x1, h2, scores_t = _outproj(x2, y_rwkv.reshape(T, -1), y_attn.reshape(T, -1), w_out[0], ln2_g[0], peer_wq[0],
                                peer_subkeys[0], tm=512)
    idx_t, gate_t = _topk(scores_t, tm=512)
    nrow = D // LANES
    c_t = _peer_act(idx_t, h2.reshape(T, nrow, LANES), gate_t, _pack_table(peer_u[0]))
    out = _peer_out(idx_t, c_t, x1.reshape(T, nrow, LANES), lnf_g, _pack_table(peer_v[0]))
    return out.reshape(B, S, D)
```

```python
import functools

import jax
import jax.numpy as jnp
import numpy as np
from jax import lax
from jax.experimental import pallas as pl
from jax.experimental.pallas import tpu as pltpu

D_MODEL = 1024
HEAD_DIM = 64
RWKV_WIDTH = 512
RWKV_HEADS = 8
ATTN_WIDTH = 512
ATTN_Q_HEADS = 8
ATTN_KV_HEADS = 2
ATTN_GROUP = 4
ATTN_KV_WIDTH = 128
WINDOW = 128
D_DECAY_LORA = 32
D_AAA_LORA = 32
D_GATE_LORA = 96
LORA_COLS = D_DECAY_LORA + D_AAA_LORA + D_GATE_LORA
LORA_PAD = 256
RKV_COLS = 3 * RWKV_WIDTH
ATTN_COLS = ATTN_WIDTH + 2 * ATTN_KV_WIDTH
PEER_HEADS = 8
PEER_N_KEYS = 128
PEER_HALF = 128
PEER_TOPK = 16
RMS_EPS = 1e-6
LNX_EPS = 64e-5
NEG_INF = -1e30

LANES = 128
SUBLANES = 8
VMEM_LIMIT_BYTES = 56 * 1024 * 1024

RWKV_CHUNK = 64
HEAD_PAIR = 2 * HEAD_DIM

_HI = lax.Precision.HIGHEST
_NN = (((1,), (0,)), ((), ()))
_NT = (((1,), (1,)), ((), ()))
_TN = (((0,), (0,)), ((), ()))


def _split(a):
    hi = a.astype(jnp.bfloat16)
    lo = (a - hi.astype(jnp.float32)).astype(jnp.bfloat16)
    return hi, lo


def _dot32(a, b, dims=_NN):
    ah, al = _split(a)
    bh, bl = _split(b)
    d = lambda p, q: lax.dot_general(p, q, dims, preferred_element_type=jnp.float32)
    return d(ah, bh) + (d(ah, bl) + d(al, bh))


def _dot16(a, b, dims=_NN):
    return lax.dot_general(a.astype(jnp.bfloat16), b.astype(jnp.bfloat16), dims,
                           preferred_element_type=jnp.float32)


def _cparams(sem):
    return pltpu.CompilerParams(dimension_semantics=sem, vmem_limit_bytes=VMEM_LIMIT_BYTES)


def _inproj_kernel(x_ref, g_ref, w_ref, b_ref, rkv_ref, lora_ref, attn_ref):
    x = x_ref[...]
    ms = jnp.mean(x * x, axis=-1, keepdims=True)
    h = (x * lax.rsqrt(ms + RMS_EPS) * g_ref[...]).astype(jnp.bfloat16)
    p = jnp.dot(h, w_ref[...], preferred_element_type=jnp.float32)
    rkv_ref[...] = p[:, :RKV_COLS]
    lora_ref[...] = p[:, RKV_COLS:RKV_COLS + LORA_PAD]
    attn_ref[...] = p[:, RKV_COLS + LORA_PAD:] + b_ref[...]


def _inproj(x2, ln1_g, w_in, b_attn, tm):
    T = x2.shape[0]
    w_rkv = w_in[:, :RKV_COLS]
    w_lora = jnp.pad(w_in[:, RKV_COLS:RKV_COLS + LORA_COLS], ((0, 0), (0, LORA_PAD - LORA_COLS)))
    w_attn = w_in[:, RKV_COLS + LORA_COLS:]
    w_all = jnp.concatenate([w_rkv, w_lora, w_attn], axis=1).astype(jnp.bfloat16)
    ncols = w_all.shape[1]
    return pl.pallas_call(
        _inproj_kernel,
        grid=(T // tm,),
        in_specs=[
            pl.BlockSpec((tm, D_MODEL), lambda i: (i, 0)),
            pl.BlockSpec((1, D_MODEL), lambda i: (0, 0)),
            pl.BlockSpec((D_MODEL, ncols), lambda i: (0, 0)),
            pl.BlockSpec((1, ATTN_COLS), lambda i: (0, 0)),
        ],
        out_specs=[
            pl.BlockSpec((tm, RKV_COLS), lambda i: (i, 0)),
            pl.BlockSpec((tm, LORA_PAD), lambda i: (i, 0)),
            pl.BlockSpec((tm, ATTN_COLS), lambda i: (i, 0)),
        ],
        out_shape=[
            jax.ShapeDtypeStruct((T, RKV_COLS), jnp.float32),
            jax.ShapeDtypeStruct((T, LORA_PAD), jnp.float32),
            jax.ShapeDtypeStruct((T, ATTN_COLS), jnp.float32),
        ],
        compiler_params=_cparams(("parallel",)),
        name="inproj",
    )(x2, ln1_g.reshape(1, D_MODEL), w_all, b_attn.reshape(1, ATTN_COLS))


def _rwkv_kernel(r_ref, k_ref, v_ref, lora_ref, mur_ref, muk_ref, muv_ref, mul_ref,
                 w0_ref, wup_ref, a0_ref, aup_ref, gup_ref, kk_ref, ka_ref, rk_ref,
                 lnw_ref, lnb_ref, o_ref, s_scr, rkv_buf, lora_buf):
    C = RWKV_CHUNK
    P = HEAD_PAIR
    f32 = jnp.float32
    c = pl.program_id(2)

    @pl.when(c == 0)
    def _():
        s_scr[...] = jnp.zeros_like(s_scr)
        rkv_buf[...] = jnp.zeros_like(rkv_buf)
        lora_buf[...] = jnp.zeros_like(lora_buf)

    def shifted(buf, j, cur, mu):
        buf[j, SUBLANES:SUBLANES + C, :] = cur
        prev = buf[j, SUBLANES - 1:SUBLANES - 1 + C, :]
        buf[j, SUBLANES - 1:SUBLANES, :] = cur[C - 1:C, :]
        return cur + mu * (prev - cur)

    r = shifted(rkv_buf, 0, r_ref[0], mur_ref[...])
    k = shifted(rkv_buf, 1, k_ref[0], muk_ref[...])
    v = shifted(rkv_buf, 2, v_ref[0], muv_ref[...])
    xl = shifted(lora_buf, 0, lora_ref[0], mul_ref[...])

    w_raw = w0_ref[...] + _dot32(jnp.tanh(xl), wup_ref[...])
    z = -w_raw
    softplus = jnp.maximum(z, 0.0) + jnp.log(1.0 + jnp.exp(-jnp.abs(z)))
    lw = -jnp.exp(-softplus - 0.5)
    a = jax.nn.sigmoid(a0_ref[...] + _dot32(xl, aup_ref[...]))
    g = _dot32(jax.nn.sigmoid(xl), gup_ref[...])

    lane = lax.broadcasted_iota(jnp.int32, (1, P), 1)
    m0 = lane < HEAD_DIM
    rowp = lax.broadcasted_iota(jnp.int32, (P, P), 0)
    colp = lax.broadcasted_iota(jnp.int32, (P, P), 1)
    same_head = (rowp < HEAD_DIM) == (colp < HEAD_DIM)
    bd = jnp.where(same_head, 1.0, 0.0).astype(f32)

    kk = k * kk_ref[...]
    nrm = jnp.sqrt(_dot32(kk * kk, bd))
    kk = kk / jnp.maximum(nrm, 1e-12)
    k2 = k * (1.0 + (a - 1.0) * ka_ref[...])
    bonus = _dot32(r * k2 * rk_ref[...], bd) * v

    ti = lax.broadcasted_iota(jnp.int32, (C, C), 0)
    tj = lax.broadcasted_iota(jnp.int32, (C, C), 1)
    incl = ti >= tj
    strict = ti > tj
    lc = _dot32(jnp.where(incl, 1.0, 0.0).astype(f32), lw)
    e_pos = jnp.exp(lc)
    e_neg = jnp.exp(-lc)
    rt = r * e_pos
    at = -kk * jnp.exp(lc - lw)
    bt = kk * a * e_neg
    kt = k2 * e_neg
    w_end = e_pos[C - 1:C, :]

    zero = jnp.zeros_like(at)
    stack = lambda t: jnp.concatenate([jnp.where(m0, t, zero), jnp.where(m0, zero, t)], axis=0)
    twice = lambda t: jnp.concatenate([t, t], axis=0)
    a2, r2, b2, k2s = stack(at), stack(rt), stack(bt), stack(kt)
    strict2 = same_head & ((rowp & (C - 1)) > (colp & (C - 1)))
    incl2 = same_head & ((rowp & (C - 1)) >= (colp & (C - 1)))
    mab = jnp.where(strict2, _dot32(a2, b2, _NT), 0.0)
    mak = jnp.where(strict2, _dot32(a2, k2s, _NT), 0.0)
    nrb = jnp.where(incl2, _dot32(r2, b2, _NT), 0.0)
    nrk = jnp.where(incl2, _dot32(r2, k2s, _NT), 0.0)
    s0 = s_scr[...]
    v2 = twice(v)
    x = twice(_dot32(at, s0, _NT)) + _dot32(mak, v2)
    m = mab
    for step in range(6):
        x = x + _dot32(m, x)
        if step < 5:
            m = _dot32(m, m)
    y2 = twice(_dot32(rt, s0, _NT)) + _dot32(jnp.concatenate([nrb, nrk], axis=1),
                                             jnp.concatenate([x, v2], axis=0))
    u = jnp.where(m0, x[:C], x[C:])
    y = jnp.where(m0, y2[:C], y2[C:])

    ds = _dot32(jnp.concatenate([u, v], axis=0), jnp.concatenate([bt, kt], axis=0), _TN)
    s_scr[...] = jnp.where(same_head, s0 + ds, 0.0) * w_end

    mean = _dot32(y, bd) * (1.0 / HEAD_DIM)
    yc = y - mean
    var = _dot32(yc * yc, bd) * (1.0 / HEAD_DIM)
    yn = yc * lax.rsqrt(var + LNX_EPS) * lnw_ref[...] + lnb_ref[...]
    o_ref[0] = (yn + bonus) * g


def _rwkv(p_rkv, p_lora, mu_shift, w0, w_up, a0, a_up, g_up, k_k, k_a, r_k, lnx_w, lnx_b):
    B, S, _ = p_rkv.shape
    C = RWKV_CHUNK
    W = RWKV_WIDTH
    npair = W // HEAD_PAIR
    row = lambda t: t.reshape(1, -1)
    mu_r, mu_k, mu_v = (row(mu_shift[i * W:(i + 1) * W]) for i in range(3))
    mu_l = row(jnp.pad(mu_shift[3 * W:], (0, LORA_PAD - LORA_COLS)))
    wup_p = jnp.zeros((LORA_PAD, W), jnp.float32).at[:D_DECAY_LORA].set(w_up)
    aup_p = jnp.zeros((LORA_PAD, W), jnp.float32).at[D_DECAY_LORA:D_DECAY_LORA + D_AAA_LORA].set(a_up)
    gup_p = jnp.zeros((LORA_PAD, W), jnp.float32).at[D_DECAY_LORA + D_AAA_LORA:LORA_COLS].set(g_up)
    vec = lambda: pl.BlockSpec((1, HEAD_PAIR), lambda b, p, c: (0, p))
    mat = lambda: pl.BlockSpec((LORA_PAD, HEAD_PAIR), lambda b, p, c: (0, p))
    col = lambda j: pl.BlockSpec((1, C, HEAD_PAIR), lambda b, p, c, j=j: (b, c, j * npair + p))
    return pl.pallas_call(
        _rwkv_kernel,
        grid=(B, npair, S // C),
        in_specs=[
            col(0), col(1), col(2),
            pl.BlockSpec((1, C, LORA_PAD), lambda b, p, c: (b, c, 0)),
            vec(), vec(), vec(),
            pl.BlockSpec((1, LORA_PAD), lambda b, p, c: (0, 0)),
            vec(), mat(), vec(), mat(), mat(), vec(), vec(), vec(), vec(), vec(),
        ],
        out_specs=pl.BlockSpec((1, C, HEAD_PAIR), lambda b, p, c: (b, c, p)),
        out_shape=jax.ShapeDtypeStruct((B, S, W), jnp.float32),
        scratch_shapes=[
            pltpu.VMEM((HEAD_PAIR, HEAD_PAIR), jnp.float32),
            pltpu.VMEM((3, SUBLANES + C, HEAD_PAIR), jnp.float32),
            pltpu.VMEM((1, SUBLANES + C, LORA_PAD), jnp.float32),
        ],
        compiler_params=_cparams(("parallel", "parallel", "arbitrary")),
        name="rwkv7",
    )(p_rkv, p_rkv, p_rkv, p_lora, mu_r, mu_k, mu_v, mu_l,
      row(w0), wup_p, row(a0), aup_p, gup_p, row(k_k), row(k_a), row(r_k), row(lnx_w), row(lnx_b))


def _attn_kernel(sink_ref, q_ref, kp_ref, kc_ref, vp_ref, vc_ref, g_ref, o_ref):
    n = pl.program_id(1)
    Wn = WINDOW
    scale = 1.0 / np.sqrt(HEAD_DIM)
    q = q_ref[0]
    kext = jnp.concatenate([kp_ref[0], kc_ref[0]], axis=0)
    vext = jnp.concatenate([vp_ref[0], vc_ref[0]], axis=0)
    qi = lax.broadcasted_iota(jnp.int32, (Wn, 2 * Wn), 0)
    kj = lax.broadcasted_iota(jnp.int32, (Wn, 2 * Wn), 1)
    diff = qi + Wn - kj
    allowed = (diff >= 0) & (diff < Wn) & ((n > 0) | (kj >= Wn))
    outs = []
    for h in range(ATTN_Q_HEADS):
        kv = h // ATTN_GROUP
        qh = q[:, h * HEAD_DIM:(h + 1) * HEAD_DIM]
        kh = kext[:, kv * HEAD_DIM:(kv + 1) * HEAD_DIM]
        vh = vext[:, kv * HEAD_DIM:(kv + 1) * HEAD_DIM]
        s = _dot16(qh, kh, _NT) * scale
        s = jnp.where(allowed, s, NEG_INF)
        sink = sink_ref[h]
        m = jnp.maximum(jnp.max(s, axis=-1, keepdims=True), sink)
        p = jnp.exp(s - m)
        denom = jnp.sum(p, axis=-1, keepdims=True) + jnp.exp(sink - m)
        outs.append(_dot16(p, vh) / denom)
    o = jnp.concatenate(outs, axis=-1)
    ms = jnp.mean(o * o, axis=-1, keepdims=True)
    o_ref[0] = o * lax.rsqrt(ms + RMS_EPS) * g_ref[...]


def _attention(p_attn, sinks, norm_g):
    B, S, _ = p_attn.shape
    Wn = WINDOW
    nb = S // Wn
    kcol = ATTN_WIDTH // ATTN_KV_WIDTH
    prev = lambda b, n: (b, jnp.maximum(n - 1, 0))
    return pl.pallas_call(
        _attn_kernel,
        grid=(B, nb),
        in_specs=[
            pl.BlockSpec(memory_space=pltpu.SMEM),
            pl.BlockSpec((1, Wn, ATTN_WIDTH), lambda b, n: (b, n, 0)),
            pl.BlockSpec((1, Wn, ATTN_KV_WIDTH), lambda b, n: prev(b, n) + (kcol,)),
            pl.BlockSpec((1, Wn, ATTN_KV_WIDTH), lambda b, n: (b, n, kcol)),
            pl.BlockSpec((1, Wn, ATTN_KV_WIDTH), lambda b, n: prev(b, n) + (kcol + 1,)),
            pl.BlockSpec((1, Wn, ATTN_KV_WIDTH), lambda b, n: (b, n, kcol + 1)),
            pl.BlockSpec((1, ATTN_WIDTH), lambda b, n: (0, 0)),
        ],
        out_specs=pl.BlockSpec((1, Wn, ATTN_WIDTH), lambda b, n: (b, n, 0)),
        out_shape=jax.ShapeDtypeStruct((B, S, ATTN_WIDTH), jnp.float32),
        compiler_params=_cparams(("parallel", "parallel")),
        name="swa_attn",
    )(sinks, p_attn, p_attn, p_attn, p_attn, p_attn, norm_g.reshape(1, ATTN_WIDTH))


def _outproj_kernel(x_ref, yr_ref, ya_ref, wr_ref, wa_ref, g_ref, wq_ref, sk_ref, x1_ref, h2_ref, st_ref):
    x1 = (x_ref[...] + _dot16(yr_ref[...], wr_ref[...]) + _dot16(ya_ref[...], wa_ref[...]))
    x1_ref[...] = x1
    ms = jnp.mean(x1 * x1, axis=-1, keepdims=True)
    h2 = x1 * lax.rsqrt(ms + RMS_EPS) * g_ref[...]
    h2_ref[...] = h2
    q = _dot16(h2, wq_ref[...])
    for hc in range(2 * PEER_HEADS):
        st_ref[hc] = _dot16(sk_ref[hc], q[:, hc * PEER_HALF:(hc + 1) * PEER_HALF], _NT)


def _outproj(x2, y_rwkv, y_attn, w_out, ln2_g, peer_wq, peer_subkeys, tm):
    T = x2.shape[0]
    nq = peer_wq.shape[1]
    nhc = 2 * PEER_HEADS
    w_r = w_out[:RWKV_WIDTH].astype(jnp.bfloat16)
    w_a = w_out[RWKV_WIDTH:].astype(jnp.bfloat16)
    sk = peer_subkeys.reshape(nhc, PEER_N_KEYS, PEER_HALF).astype(jnp.bfloat16)
    full = lambda shape: pl.BlockSpec(shape, lambda i: (0,) * len(shape))
    return pl.pallas_call(
        _outproj_kernel,
        grid=(T // tm,),
        in_specs=[
            pl.BlockSpec((tm, D_MODEL), lambda i: (i, 0)),
            pl.BlockSpec((tm, RWKV_WIDTH), lambda i: (i, 0)),
            pl.BlockSpec((tm, ATTN_WIDTH), lambda i: (i, 0)),
            full((RWKV_WIDTH, D_MODEL)), full((ATTN_WIDTH, D_MODEL)), full((1, D_MODEL)),
            full((D_MODEL, nq)), full((nhc, PEER_N_KEYS, PEER_HALF)),
        ],
        out_specs=[
            pl.BlockSpec((tm, D_MODEL), lambda i: (i, 0)),
            pl.BlockSpec((tm, D_MODEL), lambda i: (i, 0)),
            pl.BlockSpec((nhc, PEER_N_KEYS, tm), lambda i: (0, 0, i)),
        ],
        out_shape=[
            jax.ShapeDtypeStruct((T, D_MODEL), jnp.float32),
            jax.ShapeDtypeStruct((T, D_MODEL), jnp.float32),
            jax.ShapeDtypeStruct((nhc, PEER_N_KEYS, T), jnp.float32),
        ],
        compiler_params=_cparams(("parallel",)),
        name="outproj_peerq",
    )(x2, y_rwkv, y_attn, w_r, w_a, ln2_g.reshape(1, D_MODEL), peer_wq.astype(jnp.bfloat16), sk)


def _top16(s, payload=None):
    n, w = s.shape
    K = PEER_TOPK
    row = lax.broadcasted_iota(jnp.int32, (n, w), 0)
    out_row = lax.broadcasted_iota(jnp.int32, (K, w), 0)

    def body(i, carry):
        cur, vals, idxs = carry
        m = jnp.max(cur, axis=0, keepdims=True)
        am = jnp.min(jnp.where(cur == m, row, n), axis=0, keepdims=True)
        hit = row == am
        pick = am if payload is None else jnp.max(jnp.where(hit, payload, -1), axis=0, keepdims=True)
        vals = jnp.where(out_row == i, m, vals)
        idxs = jnp.where(out_row == i, pick, idxs)
        return jnp.where(hit, -jnp.inf, cur), vals, idxs

    init = (s, jnp.zeros((K, w), jnp.float32), jnp.zeros((K, w), jnp.int32))
    _, vals, idxs = lax.fori_loop(0, K, body, init, unroll=True)
    return vals, idxs


def _topk_kernel(st_ref, idx_ref, gate_ref):
    K = PEER_TOPK
    ncol = st_ref.shape[2] // LANES

    def per_col(j, _):
        col = pl.ds(pl.multiple_of(j * LANES, LANES), LANES)
        for h in range(PEER_HEADS):
            s1, i1 = _top16(st_ref[2 * h, :, col])
            s2, i2 = _top16(st_ref[2 * h + 1, :, col])
            cand_s = jnp.concatenate([s1[a:a + 1] + s2 for a in range(K)], axis=0)
            cand_i = jnp.concatenate([i1[a:a + 1] * PEER_N_KEYS + i2 for a in range(K)], axis=0)
            best_s, best_i = _top16(cand_s, cand_i)
            e = jnp.exp(best_s - best_s[0:1])
            gate = e / jnp.sum(e, axis=0, keepdims=True)
            idx_ref[h * K:(h + 1) * K, col] = best_i
            gate_ref[h * K:(h + 1) * K, col] = gate
        return 0

    lax.fori_loop(0, ncol, per_col, 0)


def _topk(scores_t, tm):
    nhc, nk, T = scores_t.shape
    ne = PEER_HEADS * PEER_TOPK
    return pl.pallas_call(
        _topk_kernel,
        grid=(T // tm,),
        in_specs=[pl.BlockSpec((nhc, nk, tm), lambda i: (0, 0, i))],
        out_specs=[pl.BlockSpec((ne, tm), lambda i: (0, i)), pl.BlockSpec((ne, tm), lambda i: (0, i))],
        out_shape=[jax.ShapeDtypeStruct((ne, T), jnp.int32), jax.ShapeDtypeStruct((ne, T), jnp.float32)],
        compiler_params=_cparams(("parallel",)),
        name="peer_topk",
    )(scores_t)


PEER_TOKEN_TILE = 128
PEER_SELECTED = PEER_HEADS * PEER_TOPK
HALF_ROWS = D_MODEL // 2 // LANES
_HI_MASK = np.uint32(0xFFFF0000)


def _pack_table(tab):
    n, d = tab.shape
    b = lax.bitcast_convert_type(tab.astype(jnp.bfloat16), jnp.uint16).astype(jnp.uint32)
    w = (b[:, :d // 2] << 16) | b[:, d // 2:]
    return w.reshape(n, d // 2 // LANES, LANES)


def _unpack(w):
    hi = lax.bitcast_convert_type(w & _HI_MASK, jnp.float32)
    lo = lax.bitcast_convert_type(w << 16, jnp.float32)
    return hi, lo


def _peer_act_kernel(idx_ref, x_ref, gate_ref, tab_ref, c_ref, prod_scr, part_scr, acc_scr):
    TB = PEER_TOKEN_TILE
    NE = PEER_SELECTED
    lane = lax.broadcasted_iota(jnp.int32, (NE, TB), 1)
    acc_scr[...] = jnp.zeros_like(acc_scr)
    part_scr[...] = jnp.zeros_like(part_scr)

    def fold(t):
        act = jnp.sum(part_scr[...], axis=-1, keepdims=True)
        acc_scr[...] = jnp.where(lane == t, act, acc_scr[...])

    def per_token(t, _):
        fold(t - 1)
        xt = x_ref[t]
        xh = xt[:HALF_ROWS]
        xl = xt[HALF_ROWS:]
        for e in range(NE):
            hi, lo = _unpack(tab_ref[idx_ref[t, e]])
            prod_scr[e * HALF_ROWS:(e + 1) * HALF_ROWS, :] = hi * xh + lo * xl
        part = prod_scr[pl.ds(0, NE, stride=HALF_ROWS), :]
        for s in range(1, HALF_ROWS):
            part = part + prod_scr[pl.ds(s, NE, stride=HALF_ROWS), :]
        part_scr[...] = part
        return 0

    lax.fori_loop(0, TB, per_token, 0)
    fold(TB - 1)
    a = acc_scr[...]
    gelu = 0.5 * a * (1.0 + lax.erf(a * np.float32(1.0 / np.sqrt(2.0))))
    c_ref[...] = (gate_ref[...] * gelu).T


def _peer_act(idx, h2_3d, gate_t, u_packed):
    T, NE = idx.shape
    TB = PEER_TOKEN_TILE
    nrow = D_MODEL // LANES
    return pl.pallas_call(
        _peer_act_kernel,
        grid=(T // TB,),
        in_specs=[
            pl.BlockSpec((TB, NE), lambda i: (i, 0), memory_space=pltpu.SMEM),
            pl.BlockSpec((TB, nrow, LANES), lambda i: (i, 0, 0)),
            pl.BlockSpec((NE, TB), lambda i: (0, i)),
            pl.BlockSpec(u_packed.shape, lambda i: (0, 0, 0), pipeline_mode=pl.Buffered(1)),
        ],
        out_specs=pl.BlockSpec((TB, NE), lambda i: (i, 0)),
        out_shape=jax.ShapeDtypeStruct((T, NE), jnp.float32),
        scratch_shapes=[pltpu.VMEM((NE * HALF_ROWS, LANES), jnp.float32), pltpu.VMEM((NE, LANES), jnp.float32),
                        pltpu.VMEM((NE, TB), jnp.float32)],
        compiler_params=_cparams(("arbitrary",)),
        name="peer_act",
    )(idx, h2_3d, gate_t, u_packed)


GATHER_STRIDE = PEER_SELECTED + SUBLANES


def _peer_out_kernel(idx_ref, c_ref, x1_ref, g_ref, tab_ref, o_ref, gat_a, gat_b):
    TB = PEER_TOKEN_TILE
    NE = PEER_SELECTED
    GS = GATHER_STRIDE
    sub = lax.broadcasted_iota(jnp.int32, (SUBLANES, NE), 0)

    def gather(buf, t):
        for e in range(NE):
            buf[pl.ds(e, HALF_ROWS, stride=GS), :] = tab_ref[idx_ref[t, e]]

    def tile(buf):
        his, los = [], []
        for s in range(HALF_ROWS):
            hi, lo = _unpack(buf[s * GS:s * GS + NE, :])
            his.append(hi.astype(jnp.bfloat16))
            los.append(lo.astype(jnp.bfloat16))
        return jnp.concatenate(his + los, axis=1)

    def per_group(gi, _):
        base = pl.multiple_of(gi * SUBLANES, SUBLANES)
        c8 = c_ref[pl.ds(base, SUBLANES), :]
        y8 = jnp.zeros((SUBLANES, D_MODEL), jnp.float32)
        for tt in range(SUBLANES):
            buf = gat_a if tt % 2 == 0 else gat_b
            gather(buf, base + tt)
            ch, cl = _split(jnp.where(sub == tt, c8, 0.0))
            out = jnp.dot(jnp.concatenate([ch, cl], axis=0), tile(buf), preferred_element_type=jnp.float32)
            y8 = y8 + (out[:SUBLANES] + out[SUBLANES:])
        z = x1_ref[pl.ds(base, SUBLANES), :] + y8
        ms = jnp.mean(z * z, axis=-1, keepdims=True)
        o_ref[pl.ds(base, SUBLANES), :] = z * lax.rsqrt(ms + RMS_EPS) * g_ref[...]
        return 0

    lax.fori_loop(0, TB // SUBLANES, per_group, 0)


def _peer_out(idx, c, x1, lnf_g, v_packed):
    T, NE = idx.shape
    TB = PEER_TOKEN_TILE
    gat = pltpu.VMEM((HALF_ROWS * GATHER_STRIDE, LANES), jnp.uint32)
    return pl.pallas_call(
        _peer_out_kernel,
        grid=(T // TB,),
        in_specs=[
            pl.BlockSpec((TB, NE), lambda i: (i, 0), memory_space=pltpu.SMEM),
            pl.BlockSpec((TB, NE), lambda i: (i, 0)),
            pl.BlockSpec((TB, D_MODEL), lambda i: (i, 0)),
            pl.BlockSpec((1, D_MODEL), lambda i: (0, 0)),
            pl.BlockSpec(v_packed.shape, lambda i: (0, 0, 0), pipeline_mode=pl.Buffered(1)),
        ],
        out_specs=pl.BlockSpec((TB, D_MODEL), lambda i: (i, 0)),
        out_shape=jax.ShapeDtypeStruct((T, D_MODEL), jnp.float32),
        scratch_shapes=[gat, gat],
        compiler_params=_cparams(("arbitrary",)),
        name="peer_out",
    )(idx, c, x1, lnf_g.reshape(1, D_MODEL), v_packed)


def kernel(x, ln1_g, w_in, b_attn, mu_shift, w0, w_up, a0, a_up, g_up, k_k, k_a, r_k, lnx_w, lnx_b, attn_sinks, attn_norm_g, w_out, ln2_g, peer_wq, peer_subkeys, peer_u, peer_v, lnf_g):
    B, S, D = x.shape
    T = B * S
    x2 = x.reshape(T, D)
    p_rkv, p_lora, p_attn = _inproj(x2, ln1_g[0], w_in[0], b_attn[0], tm=512)
    y_rwkv = _rwkv(p_rkv.reshape(B, S, -1), p_lora.reshape(B, S, -1), mu_shift[0], w0[0], w_up[0], a0[0],
                   a_up[0], g_up[0], k_k[0], k_a[0], r_k[0].reshape(-1), lnx_w[0], lnx_b[0])
    y_attn = _attention(p_attn.reshape(B, S, -1), attn_sinks[0], attn_norm_g[0])
    x1, h2, scores_t = _outproj(x2, y_rwkv.reshape(T, -1), y_attn.reshape(T, -1), w_out[0], ln2_g[0], peer_wq[0],
                                peer_subkeys[0], tm=512)
    idx_t, gate_t = _topk(scores_t, tm=512)
    nrow = D // LANES
    idx = idx_t.T
    c = _peer_act(idx, h2.reshape(T, nrow, LANES), gate_t, _pack_table(peer_u[0]))
    out = _peer_out(idx, c, x1, lnf_g, _pack_table(peer_v[0]))
    return out.reshape(B, S, D)
```

```python
import functools

import jax
import jax.numpy as jnp
import numpy as np
from jax import lax
from jax.experimental import pallas as pl
from jax.experimental.pallas import tpu as pltpu

D_MODEL = 1024
HEAD_DIM = 64
RWKV_WIDTH = 512
RWKV_HEADS = 8
ATTN_WIDTH = 512
ATTN_Q_HEADS = 8
ATTN_KV_HEADS = 2
ATTN_GROUP = 4
ATTN_KV_WIDTH = 128
WINDOW = 128
D_DECAY_LORA = 32
D_AAA_LORA = 32
D_GATE_LORA = 96
LORA_COLS = D_DECAY_LORA + D_AAA_LORA + D_GATE_LORA
LORA_PAD = 256
RKV_COLS = 3 * RWKV_WIDTH
ATTN_COLS = ATTN_WIDTH + 2 * ATTN_KV_WIDTH
PEER_HEADS = 8
PEER_N_KEYS = 128
PEER_HALF = 128
PEER_TOPK = 16
RMS_EPS = 1e-6
LNX_EPS = 64e-5
NEG_INF = -1e30

LANES = 128
SUBLANES = 8
VMEM_LIMIT_BYTES = 56 * 1024 * 1024

RWKV_CHUNK = 64
HEAD_PAIR = 2 * HEAD_DIM

_HI = lax.Precision.HIGHEST
_NN = (((1,), (0,)), ((), ()))
_NT = (((1,), (1,)), ((), ()))
_TN = (((0,), (0,)), ((), ()))


def _split(a):
    hi = a.astype(jnp.bfloat16)
    lo = (a - hi.astype(jnp.float32)).astype(jnp.bfloat16)
    return hi, lo


def _dot32(a, b, dims=_NN):
    ah, al = _split(a)
    bh, bl = _split(b)
    d = lambda p, q: lax.dot_general(p, q, dims, preferred_element_type=jnp.float32)
    return d(ah, bh) + (d(ah, bl) + d(al, bh))


def _dot16(a, b, dims=_NN):
    return lax.dot_general(a.astype(jnp.bfloat16), b.astype(jnp.bfloat16), dims,
                           preferred_element_type=jnp.float32)


def _cparams(sem):
    return pltpu.CompilerParams(dimension_semantics=sem, vmem_limit_bytes=VMEM_LIMIT_BYTES)


def _inproj_kernel(x_ref, g_ref, w_ref, b_ref, rkv_ref, lora_ref, attn_ref):
    x = x_ref[...]
    ms = jnp.mean(x * x, axis=-1, keepdims=True)
    h = (x * lax.rsqrt(ms + RMS_EPS) * g_ref[...]).astype(jnp.bfloat16)
    p = jnp.dot(h, w_ref[...], preferred_element_type=jnp.float32)
    rkv_ref[...] = p[:, :RKV_COLS]
    lora_ref[...] = p[:, RKV_COLS:RKV_COLS + LORA_PAD]
    attn_ref[...] = p[:, RKV_COLS + LORA_PAD:] + b_ref[...]


def _inproj(x2, ln1_g, w_in, b_attn, tm):
    T = x2.shape[0]
    w_rkv = w_in[:, :RKV_COLS]
    w_lora = jnp.pad(w_in[:, RKV_COLS:RKV_COLS + LORA_COLS], ((0, 0), (0, LORA_PAD - LORA_COLS)))
    w_attn = w_in[:, RKV_COLS + LORA_COLS:]
    w_all = jnp.concatenate([w_rkv, w_lora, w_attn], axis=1).astype(jnp.bfloat16)
    ncols = w_all.shape[1]
    return pl.pallas_call(
        _inproj_kernel,
        grid=(T // tm,),
        in_specs=[
            pl.BlockSpec((tm, D_MODEL), lambda i: (i, 0)),
            pl.BlockSpec((1, D_MODEL), lambda i: (0, 0)),
            pl.BlockSpec((D_MODEL, ncols), lambda i: (0, 0)),
            pl.BlockSpec((1, ATTN_COLS), lambda i: (0, 0)),
        ],
        out_specs=[
            pl.BlockSpec((tm, RKV_COLS), lambda i: (i, 0)),
            pl.BlockSpec((tm, LORA_PAD), lambda i: (i, 0)),
            pl.BlockSpec((tm, ATTN_COLS), lambda i: (i, 0)),
        ],
        out_shape=[
            jax.ShapeDtypeStruct((T, RKV_COLS), jnp.float32),
            jax.ShapeDtypeStruct((T, LORA_PAD), jnp.float32),
            jax.ShapeDtypeStruct((T, ATTN_COLS), jnp.float32),
        ],
        compiler_params=_cparams(("parallel",)),
        name="inproj",
    )(x2, ln1_g.reshape(1, D_MODEL), w_all, b_attn.reshape(1, ATTN_COLS))


def _rwkv_kernel(r_ref, k_ref, v_ref, lora_ref, mur_ref, muk_ref, muv_ref, mul_ref,
                 w0_ref, wup_ref, a0_ref, aup_ref, gup_ref, kk_ref, ka_ref, rk_ref,
                 lnw_ref, lnb_ref, o_ref, s_scr, rkv_buf, lora_buf):
    C = RWKV_CHUNK
    P = HEAD_PAIR
    f32 = jnp.float32
    c = pl.program_id(1)

    @pl.when(c == 0)
    def _():
        s_scr[...] = jnp.zeros_like(s_scr)
        rkv_buf[...] = jnp.zeros_like(rkv_buf)
        lora_buf[...] = jnp.zeros_like(lora_buf)

    def shifted(buf, j, cur, mu):
        buf[j, SUBLANES:SUBLANES + C, :] = cur
        prev = buf[j, SUBLANES - 1:SUBLANES - 1 + C, :]
        buf[j, SUBLANES - 1:SUBLANES, :] = cur[C - 1:C, :]
        return cur + mu * (prev - cur)

    r = shifted(rkv_buf, 0, r_ref[0], mur_ref[...])
    k = shifted(rkv_buf, 1, k_ref[0], muk_ref[...])
    v = shifted(rkv_buf, 2, v_ref[0], muv_ref[...])
    xl = shifted(lora_buf, 0, lora_ref[0], mul_ref[...])

    w_raw = w0_ref[...] + _dot16(jnp.tanh(xl), wup_ref[...])
    z = -w_raw
    softplus = jnp.maximum(z, 0.0) + jnp.log(1.0 + jnp.exp(-jnp.abs(z)))
    lw = -jnp.exp(-softplus - 0.5)
    a = jax.nn.sigmoid(a0_ref[...] + _dot16(xl, aup_ref[...]))
    g = _dot32(jax.nn.sigmoid(xl), gup_ref[...])

    lane = lax.broadcasted_iota(jnp.int32, (1, P), 1)
    m0 = lane < HEAD_DIM
    rowp = lax.broadcasted_iota(jnp.int32, (P, P), 0)
    colp = lax.broadcasted_iota(jnp.int32, (P, P), 1)
    same_head = (rowp < HEAD_DIM) == (colp < HEAD_DIM)
    bd = jnp.where(same_head, 1.0, 0.0).astype(jnp.bfloat16)
    strict2 = same_head & ((rowp & (C - 1)) > (colp & (C - 1)))
    incl2 = same_head & ((rowp & (C - 1)) >= (colp & (C - 1)))
    ti = lax.broadcasted_iota(jnp.int32, (C, C), 0)
    tj = lax.broadcasted_iota(jnp.int32, (C, C), 1)
    tril = jnp.where(ti >= tj, 1.0, 0.0).astype(f32)
    head_sum = lambda t: jnp.concatenate(
        [_dot16(t[:, p * P:(p + 1) * P], bd) for p in range(t.shape[1] // P)], axis=1)

    kk = k * kk_ref[...]
    kk = kk / jnp.maximum(jnp.sqrt(head_sum(kk * kk)), 1e-12)
    k2 = k * (1.0 + (a - 1.0) * ka_ref[...])
    bonus = head_sum(r * k2 * rk_ref[...]) * v

    lc = _dot32(tril, lw)
    e_pos = jnp.exp(lc)
    e_neg = jnp.exp(-lc)
    rt_all = r * e_pos
    at_all = -kk * jnp.exp(lc - lw)
    bt_all = kk * a * e_neg
    kt_all = k2 * e_neg

    twice = lambda t: jnp.concatenate([t, t], axis=0)
    stack = lambda t: jnp.concatenate([jnp.where(m0, t, 0.0), jnp.where(m0, 0.0, t)], axis=0)
    pairs = range(r.shape[1] // P)
    cut = lambda t: [t[:, p * P:(p + 1) * P] for p in pairs]
    at, rt, bt, kt, vp = cut(at_all), cut(rt_all), cut(bt_all), cut(kt_all), cut(v)
    s0 = [s_scr[p] for p in pairs]
    bk = [jnp.concatenate([stack(bt[p]), stack(kt[p])], axis=0) for p in pairs]
    pa = [_dot16(stack(at[p]), bk[p], _NT) for p in pairs]
    m = [jnp.where(strict2, pa[p][:, :P], 0.0) for p in pairs]
    v2 = [twice(vp[p]) for p in pairs]
    x = [twice(_dot16(at[p], s0[p], _NT)) + _dot16(jnp.where(strict2, pa[p][:, P:], 0.0), v2[p])
         for p in pairs]
    for step in range(6):
        x = [x[p] + _dot16(m[p], x[p]) for p in pairs]
        if step < 5:
            m = [_dot16(m[p], m[p]) for p in pairs]
    pr = [_dot32(stack(rt[p]), bk[p], _NT) for p in pairs]
    nr = [jnp.concatenate([jnp.where(incl2, pr[p][:, :P], 0.0), jnp.where(incl2, pr[p][:, P:], 0.0)], axis=1)
          for p in pairs]
    y2 = [twice(_dot16(rt[p], s0[p], _NT)) + _dot32(nr[p], jnp.concatenate([x[p], v2[p]], axis=0))
          for p in pairs]
    u = [jnp.where(m0, x[p][:C], x[p][C:]) for p in pairs]
    ys = [jnp.where(m0, y2[p][:C], y2[p][C:]) for p in pairs]
    for p in pairs:
        ds = _dot32(jnp.concatenate([u[p], vp[p]], axis=0), jnp.concatenate([bt[p], kt[p]], axis=0), _TN)
        s_scr[p] = jnp.where(same_head, s0[p] + ds, 0.0) * e_pos[C - 1:C, p * P:(p + 1) * P]
    y = jnp.concatenate(ys, axis=1)

    mean = head_sum(y) * (1.0 / HEAD_DIM)
    yc = y - mean
    var = head_sum(yc * yc) * (1.0 / HEAD_DIM)
    yn = yc * lax.rsqrt(var + LNX_EPS) * lnw_ref[...] + lnb_ref[...]
    o_ref[0] = (yn + bonus) * g


def _rwkv(p_rkv, p_lora, mu_shift, w0, w_up, a0, a_up, g_up, k_k, k_a, r_k, lnx_w, lnx_b):
    B, S, _ = p_rkv.shape
    C = RWKV_CHUNK
    W = RWKV_WIDTH
    row = lambda t: t.reshape(1, -1)
    mu_r, mu_k, mu_v = (row(mu_shift[i * W:(i + 1) * W]) for i in range(3))
    mu_l = row(jnp.pad(mu_shift[3 * W:], (0, LORA_PAD - LORA_COLS)))
    wup_p = jnp.zeros((LORA_PAD, W), jnp.float32).at[:D_DECAY_LORA].set(w_up)
    aup_p = jnp.zeros((LORA_PAD, W), jnp.float32).at[D_DECAY_LORA:D_DECAY_LORA + D_AAA_LORA].set(a_up)
    gup_p = jnp.zeros((LORA_PAD, W), jnp.float32).at[D_DECAY_LORA + D_AAA_LORA:LORA_COLS].set(g_up)
    vec = lambda: pl.BlockSpec((1, W), lambda b, c: (0, 0))
    mat = lambda: pl.BlockSpec((LORA_PAD, W), lambda b, c: (0, 0))
    col = lambda j: pl.BlockSpec((1, C, W), lambda b, c, j=j: (b, c, j))
    return pl.pallas_call(
        _rwkv_kernel,
        grid=(B, S // C),
        in_specs=[
            col(0), col(1), col(2),
            pl.BlockSpec((1, C, LORA_PAD), lambda b, c: (b, c, 0)),
            vec(), vec(), vec(),
            pl.BlockSpec((1, LORA_PAD), lambda b, c: (0, 0)),
            vec(), mat(), vec(), mat(), mat(), vec(), vec(), vec(), vec(), vec(),
        ],
        out_specs=pl.BlockSpec((1, C, W), lambda b, c: (b, c, 0)),
        out_shape=jax.ShapeDtypeStruct((B, S, W), jnp.float32),
        scratch_shapes=[
            pltpu.VMEM((W // HEAD_PAIR, HEAD_PAIR, HEAD_PAIR), jnp.float32),
            pltpu.VMEM((3, SUBLANES + C, W), jnp.float32),
            pltpu.VMEM((1, SUBLANES + C, LORA_PAD), jnp.float32),
        ],
        compiler_params=_cparams(("parallel", "arbitrary")),
        name="rwkv7",
    )(p_rkv, p_rkv, p_rkv, p_lora, mu_r, mu_k, mu_v, mu_l,
      row(w0), wup_p, row(a0), aup_p, gup_p, row(k_k), row(k_a), row(r_k), row(lnx_w), row(lnx_b))


def _attn_kernel(sink_ref, q_ref, kp_ref, kc_ref, vp_ref, vc_ref, g_ref, o_ref):
    n = pl.program_id(1)
    Wn = WINDOW
    scale = 1.0 / np.sqrt(HEAD_DIM)
    q = q_ref[0]
    kext = jnp.concatenate([kp_ref[0], kc_ref[0]], axis=0)
    vext = jnp.concatenate([vp_ref[0], vc_ref[0]], axis=0)
    qi = lax.broadcasted_iota(jnp.int32, (Wn, 2 * Wn), 0)
    kj = lax.broadcasted_iota(jnp.int32, (Wn, 2 * Wn), 1)
    diff = qi + Wn - kj
    allowed = (diff >= 0) & (diff < Wn) & ((n > 0) | (kj >= Wn))
    outs = []
    for h in range(ATTN_Q_HEADS):
        kv = h // ATTN_GROUP
        qh = q[:, h * HEAD_DIM:(h + 1) * HEAD_DIM]
        kh = kext[:, kv * HEAD_DIM:(kv + 1) * HEAD_DIM]
        vh = vext[:, kv * HEAD_DIM:(kv + 1) * HEAD_DIM]
        s = _dot16(qh, kh, _NT) * scale
        s = jnp.where(allowed, s, NEG_INF)
        sink = sink_ref[h]
        m = jnp.maximum(jnp.max(s, axis=-1, keepdims=True), sink)
        p = jnp.exp(s - m)
        denom = jnp.sum(p, axis=-1, keepdims=True) + jnp.exp(sink - m)
        outs.append(_dot16(p, vh) / denom)
    o = jnp.concatenate(outs, axis=-1)
    ms = jnp.mean(o * o, axis=-1, keepdims=True)
    o_ref[0] = o * lax.rsqrt(ms + RMS_EPS) * g_ref[...]


def _attention(p_attn, sinks, norm_g):
    B, S, _ = p_attn.shape
    Wn = WINDOW
    nb = S // Wn
    kcol = ATTN_WIDTH // ATTN_KV_WIDTH
    prev = lambda b, n: (b, jnp.maximum(n - 1, 0))
    return pl.pallas_call(
        _attn_kernel,
        grid=(B, nb),
        in_specs=[
            pl.BlockSpec(memory_space=pltpu.SMEM),
            pl.BlockSpec((1, Wn, ATTN_WIDTH), lambda b, n: (b, n, 0)),
            pl.BlockSpec((1, Wn, ATTN_KV_WIDTH), lambda b, n: prev(b, n) + (kcol,)),
            pl.BlockSpec((1, Wn, ATTN_KV_WIDTH), lambda b, n: (b, n, kcol)),
            pl.BlockSpec((1, Wn, ATTN_KV_WIDTH), lambda b, n: prev(b, n) + (kcol + 1,)),
            pl.BlockSpec((1, Wn, ATTN_KV_WIDTH), lambda b, n: (b, n, kcol + 1)),
            pl.BlockSpec((1, ATTN_WIDTH), lambda b, n: (0, 0)),
        ],
        out_specs=pl.BlockSpec((1, Wn, ATTN_WIDTH), lambda b, n: (b, n, 0)),
        out_shape=jax.ShapeDtypeStruct((B, S, ATTN_WIDTH), jnp.float32),
        compiler_params=_cparams(("parallel", "parallel")),
        name="swa_attn",
    )(sinks, p_attn, p_attn, p_attn, p_attn, p_attn, norm_g.reshape(1, ATTN_WIDTH))


def _outproj_kernel(x_ref, yr_ref, ya_ref, wr_ref, wa_ref, g_ref, wq_ref, sk_ref, x1_ref, h2_ref, st_ref):
    x1 = (x_ref[...] + _dot16(yr_ref[...], wr_ref[...]) + _dot16(ya_ref[...], wa_ref[...]))
    x1_ref[...] = x1
    ms = jnp.mean(x1 * x1, axis=-1, keepdims=True)
    h2 = x1 * lax.rsqrt(ms + RMS_EPS) * g_ref[...]
    h2_ref[...] = h2
    q = _dot16(h2, wq_ref[...])
    for hc in range(2 * PEER_HEADS):
        st_ref[hc] = _dot16(sk_ref[hc], q[:, hc * PEER_HALF:(hc + 1) * PEER_HALF], _NT)


def _outproj(x2, y_rwkv, y_attn, w_out, ln2_g, peer_wq, peer_subkeys, tm):
    T = x2.shape[0]
    nq = peer_wq.shape[1]
    nhc = 2 * PEER_HEADS
    w_r = w_out[:RWKV_WIDTH].astype(jnp.bfloat16)
    w_a = w_out[RWKV_WIDTH:].astype(jnp.bfloat16)
    sk = peer_subkeys.reshape(nhc, PEER_N_KEYS, PEER_HALF).astype(jnp.bfloat16)
    full = lambda shape: pl.BlockSpec(shape, lambda i: (0,) * len(shape))
    return pl.pallas_call(
        _outproj_kernel,
        grid=(T // tm,),
        in_specs=[
            pl.BlockSpec((tm, D_MODEL), lambda i: (i, 0)),
            pl.BlockSpec((tm, RWKV_WIDTH), lambda i: (i, 0)),
            pl.BlockSpec((tm, ATTN_WIDTH), lambda i: (i, 0)),
            full((RWKV_WIDTH, D_MODEL)), full((ATTN_WIDTH, D_MODEL)), full((1, D_MODEL)),
            full((D_MODEL, nq)), full((nhc, PEER_N_KEYS, PEER_HALF)),
        ],
        out_specs=[
            pl.BlockSpec((tm, D_MODEL), lambda i: (i, 0)),
            pl.BlockSpec((tm, D_MODEL), lambda i: (i, 0)),
            pl.BlockSpec((nhc, PEER_N_KEYS, tm), lambda i: (0, 0, i)),
        ],
        out_shape=[
            jax.ShapeDtypeStruct((T, D_MODEL), jnp.float32),
            jax.ShapeDtypeStruct((T, D_MODEL), jnp.float32),
            jax.ShapeDtypeStruct((nhc, PEER_N_KEYS, T), jnp.float32),
        ],
        compiler_params=_cparams(("parallel",)),
        name="outproj_peerq",
    )(x2, y_rwkv, y_attn, w_r, w_a, ln2_g.reshape(1, D_MODEL), peer_wq.astype(jnp.bfloat16), sk)


def _top16(s, order, payload=None):
    n, w = s.shape
    K = PEER_TOPK
    payload = order if payload is None else payload
    out_row = lax.broadcasted_iota(jnp.int32, (K, w), 0)
    cur = s
    vals = jnp.zeros((K, w), jnp.float32)
    picks = jnp.zeros((K, w), jnp.float32)
    for i in range(K):
        m = jnp.max(cur, axis=0, keepdims=True)
        first = jnp.min(jnp.where(cur == m, order, np.float32(1e9)), axis=0, keepdims=True)
        hit = order == first
        pick = first if payload is order else jnp.max(jnp.where(hit, payload, -1.0), axis=0, keepdims=True)
        vals = jnp.where(out_row == i, m, vals)
        picks = jnp.where(out_row == i, pick, picks)
        cur = jnp.where(hit, -jnp.inf, cur)
    return vals, picks


def _staircase():
    K = PEER_TOPK
    pairs = [(a, b) for a in range(K) for b in range(K) if (a + 1) * (b + 1) <= K]
    n = -(-len(pairs) // SUBLANES) * SUBLANES
    sel = np.zeros((2, n, K), np.float32)
    pos = np.full((n, LANES), -1.0, np.float32)
    for r, (a, b) in enumerate(pairs):
        sel[0, r, a] = 1.0
        sel[1, r, b] = 1.0
        pos[r] = a * K + b
    return sel, pos


def _split3(x):
    h1 = x.astype(jnp.bfloat16)
    r1 = x - h1.astype(jnp.float32)
    h2 = r1.astype(jnp.bfloat16)
    h3 = (r1 - h2.astype(jnp.float32)).astype(jnp.bfloat16)
    return h1, h2, h3


def _pick_rows(sel, x):
    d = lambda t: jnp.dot(sel, t, preferred_element_type=jnp.float32)
    h1, h2, h3 = _split3(x)
    return (d(h1) + d(h2)) + d(h3)


def _topk_kernel(st_ref, sel_ref, pos_ref, idx_ref, gate_ref):
    K = PEER_TOPK
    ncol = st_ref.shape[2] // LANES
    key_order = lax.broadcasted_iota(jnp.int32, (PEER_N_KEYS, LANES), 0).astype(jnp.float32)
    sel_a = sel_ref[0]
    sel_b = sel_ref[1]
    pos = pos_ref[...]
    live = pos >= 0.0

    def per_col(j, _):
        col = pl.ds(pl.multiple_of(j * LANES, LANES), LANES)
        for h in range(PEER_HEADS):
            s1, i1 = _top16(st_ref[2 * h, :, col], key_order)
            s2, i2 = _top16(st_ref[2 * h + 1, :, col], key_order)
            cand_s = jnp.where(live, _pick_rows(sel_a, s1) + _pick_rows(sel_b, s2), -jnp.inf)
            pick = lambda sel, t: jnp.dot(sel, t.astype(jnp.bfloat16), preferred_element_type=jnp.float32)
            cand_i = pick(sel_a, i1) * np.float32(PEER_N_KEYS) + pick(sel_b, i2)
            best_s, best_i = _top16(cand_s, pos, cand_i)
            e = jnp.exp(best_s - best_s[0:1])
            gate = e / jnp.sum(e, axis=0, keepdims=True)
            idx_ref[h * K:(h + 1) * K, col] = best_i.astype(jnp.int32)
            gate_ref[h * K:(h + 1) * K, col] = gate
        return 0

    lax.fori_loop(0, ncol, per_col, 0)


def _topk(scores_t, tm):
    nhc, nk, T = scores_t.shape
    ne = PEER_HEADS * PEER_TOPK
    sel, pos = _staircase()
    return pl.pallas_call(
        _topk_kernel,
        grid=(T // tm,),
        in_specs=[pl.BlockSpec((nhc, nk, tm), lambda i: (0, 0, i)),
                  pl.BlockSpec(sel.shape, lambda i: (0, 0, 0)),
                  pl.BlockSpec(pos.shape, lambda i: (0, 0))],
        out_specs=[pl.BlockSpec((ne, tm), lambda i: (0, i)), pl.BlockSpec((ne, tm), lambda i: (0, i))],
        out_shape=[jax.ShapeDtypeStruct((ne, T), jnp.int32), jax.ShapeDtypeStruct((ne, T), jnp.float32)],
        compiler_params=_cparams(("parallel",)),
        name="peer_topk",
    )(scores_t, jnp.asarray(sel, jnp.bfloat16), jnp.asarray(pos))


PEER_TOKEN_TILE = 128
PEER_SELECTED = PEER_HEADS * PEER_TOPK
HALF_ROWS = D_MODEL // 2 // LANES
_HI_MASK = np.uint32(0xFFFF0000)


def _pack_table(tab):
    n, d = tab.shape
    b = lax.bitcast_convert_type(tab.astype(jnp.bfloat16), jnp.uint16).astype(jnp.uint32)
    w = (b[:, :d // 2] << 16) | b[:, d // 2:]
    return w.reshape(n, d // 2 // LANES, LANES)


def _unpack(w):
    hi = lax.bitcast_convert_type(w & _HI_MASK, jnp.float32)
    lo = lax.bitcast_convert_type(w << 16, jnp.float32)
    return hi, lo


def _peer_act_kernel(idx_ref, x_ref, gate_ref, tab_ref, c_ref, prod_scr, part_scr, acc_scr):
    TB = PEER_TOKEN_TILE
    NE = PEER_SELECTED
    lane = lax.broadcasted_iota(jnp.int32, (NE, TB), 1)
    acc_scr[...] = jnp.zeros_like(acc_scr)
    part_scr[...] = jnp.zeros_like(part_scr)

    def fold(t):
        act = jnp.sum(part_scr[...], axis=-1, keepdims=True)
        acc_scr[...] = jnp.where(lane == t, act, acc_scr[...])

    def per_token(t, _):
        fold(t - 1)
        xt = x_ref[t]
        xh = xt[:HALF_ROWS]
        xl = xt[HALF_ROWS:]
        for e in range(NE):
            hi, lo = _unpack(tab_ref[idx_ref[t, e]])
            prod_scr[e * HALF_ROWS:(e + 1) * HALF_ROWS, :] = hi * xh + lo * xl
        part = prod_scr[pl.ds(0, NE, stride=HALF_ROWS), :]
        for s in range(1, HALF_ROWS):
            part = part + prod_scr[pl.ds(s, NE, stride=HALF_ROWS), :]
        part_scr[...] = part
        return 0

    lax.fori_loop(0, TB, per_token, 0)
    fold(TB - 1)
    a = acc_scr[...]
    gelu = 0.5 * a * (1.0 + lax.erf(a * np.float32(1.0 / np.sqrt(2.0))))
    c_ref[...] = (gate_ref[...] * gelu).T


def _peer_act(idx, h2_3d, gate_t, u_packed):
    T, NE = idx.shape
    TB = PEER_TOKEN_TILE
    nrow = D_MODEL // LANES
    return pl.pallas_call(
        _peer_act_kernel,
        grid=(T // TB,),
        in_specs=[
            pl.BlockSpec((TB, NE), lambda i: (i, 0), memory_space=pltpu.SMEM),
            pl.BlockSpec((TB, nrow, LANES), lambda i: (i, 0, 0)),
            pl.BlockSpec((NE, TB), lambda i: (0, i)),
            pl.BlockSpec(u_packed.shape, lambda i: (0, 0, 0), pipeline_mode=pl.Buffered(1)),
        ],
        out_specs=pl.BlockSpec((TB, NE), lambda i: (i, 0)),
        out_shape=jax.ShapeDtypeStruct((T, NE), jnp.float32),
        scratch_shapes=[pltpu.VMEM((NE * HALF_ROWS, LANES), jnp.float32), pltpu.VMEM((NE, LANES), jnp.float32),
                        pltpu.VMEM((NE, TB), jnp.float32)],
        compiler_params=_cparams(("arbitrary",)),
        name="peer_act",
    )(idx, h2_3d, gate_t, u_packed)


GATHER_STRIDE = PEER_SELECTED + SUBLANES


def _peer_out_kernel(idx_ref, c_ref, x1_ref, g_ref, tab_ref, o_ref, gat_a, gat_b):
    TB = PEER_TOKEN_TILE
    NE = PEER_SELECTED
    GS = GATHER_STRIDE
    sub = lax.broadcasted_iota(jnp.int32, (SUBLANES, NE), 0)

    def gather(buf, t):
        for e in range(NE):
            buf[pl.ds(e, HALF_ROWS, stride=GS), :] = tab_ref[idx_ref[t, e]]

    def tile(buf):
        his, los = [], []
        for s in range(HALF_ROWS):
            hi, lo = _unpack(buf[s * GS:s * GS + NE, :])
            his.append(hi.astype(jnp.bfloat16))
            los.append(lo.astype(jnp.bfloat16))
        return jnp.concatenate(his + los, axis=1)

    def per_group(gi, _):
        base = pl.multiple_of(gi * SUBLANES, SUBLANES)
        c8 = c_ref[pl.ds(base, SUBLANES), :]
        y8 = jnp.zeros((SUBLANES, D_MODEL), jnp.float32)
        for tt in range(SUBLANES):
            buf = gat_a if tt % 2 == 0 else gat_b
            gather(buf, base + tt)
            ch, cl = _split(jnp.where(sub == tt, c8, 0.0))
            out = jnp.dot(jnp.concatenate([ch, cl], axis=0), tile(buf), preferred_element_type=jnp.float32)
            y8 = y8 + (out[:SUBLANES] + out[SUBLANES:])
        z = x1_ref[pl.ds(base, SUBLANES), :] + y8
        ms = jnp.mean(z * z, axis=-1, keepdims=True)
        o_ref[pl.ds(base, SUBLANES), :] = z * lax.rsqrt(ms + RMS_EPS) * g_ref[...]
        return 0

    lax.fori_loop(0, TB // SUBLANES, per_group, 0)


def _peer_out(idx, c, x1, lnf_g, v_packed):
    T, NE = idx.shape
    TB = PEER_TOKEN_TILE
    gat = pltpu.VMEM((HALF_ROWS * GATHER_STRIDE, LANES), jnp.uint32)
    return pl.pallas_call(
        _peer_out_kernel,
        grid=(T // TB,),
        in_specs=[
            pl.BlockSpec((TB, NE), lambda i: (i, 0), memory_space=pltpu.SMEM),
            pl.BlockSpec((TB, NE), lambda i: (i, 0)),
            pl.BlockSpec((TB, D_MODEL), lambda i: (i, 0)),
            pl.BlockSpec((1, D_MODEL), lambda i: (0, 0)),
            pl.BlockSpec(v_packed.shape, lambda i: (0, 0, 0), pipeline_mode=pl.Buffered(1)),
        ],
        out_specs=pl.BlockSpec((TB, D_MODEL), lambda i: (i, 0)),
        out_shape=jax.ShapeDtypeStruct((T, D_MODEL), jnp.float32),
        scratch_shapes=[gat, gat],
        compiler_params=_cparams(("arbitrary",)),
        name="peer_out",
    )(idx, c, x1, lnf_g.reshape(1, D_MODEL), v_packed)


def kernel(x, ln1_g, w_in, b_attn, mu_shift, w0, w_up, a0, a_up, g_up, k_k, k_a, r_k, lnx_w, lnx_b, attn_sinks, attn_norm_g, w_out, ln2_g, peer_wq, peer_subkeys, peer_u, peer_v, lnf_g):
    B, S, D = x.shape
    T = B * S
    x2 = x.reshape(T, D)
    p_rkv, p_lora, p_attn = _inproj(x2, ln1_g[0], w_in[0], b_attn[0], tm=512)
    y_rwkv = _rwkv(p_rkv.reshape(B, S, -1), p_lora.reshape(B, S, -1), mu_shift[0], w0[0], w_up[0], a0[0],
                   a_up[0], g_up[0], k_k[0], k_a[0], r_k[0].reshape(-1), lnx_w[0], lnx_b[0])
    y_attn = _attention(p_attn.reshape(B, S, -1), attn_sinks[0], attn_norm_g[0])
    x1, h2, scores_t = _outproj(x2, y_rwkv.reshape(T, -1), y_attn.reshape(T, -1), w_out[0], ln2_g[0], peer_wq[0],
                                peer_subkeys[0], tm=512)
    idx_t, gate_t = _topk(scores_t, tm=512)
    nrow = D // LANES
    idx = idx_t.T
    c = _peer_act(idx, h2.reshape(T, nrow, LANES), gate_t, _pack_table(peer_u[0]))
    out = _peer_out(idx, c, x1, lnf_g, _pack_table(peer_v[0]))
    return out.reshape(B, S, D)
```

```python
import functools

import jax
import jax.numpy as jnp
import numpy as np
from jax import lax
from jax.experimental import pallas as pl
from jax.experimental.pallas import tpu as pltpu
from jax.experimental.pallas import tpu_sc as plsc

D_MODEL = 1024
HEAD_DIM = 64
RWKV_WIDTH = 512
RWKV_HEADS = 8
ATTN_WIDTH = 512
ATTN_Q_HEADS = 8
ATTN_KV_HEADS = 2
ATTN_GROUP = 4
ATTN_KV_WIDTH = 128
WINDOW = 128
D_DECAY_LORA = 32
D_AAA_LORA = 32
D_GATE_LORA = 96
LORA_COLS = D_DECAY_LORA + D_AAA_LORA + D_GATE_LORA
LORA_PAD = 256
RKV_COLS = 3 * RWKV_WIDTH
ATTN_COLS = ATTN_WIDTH + 2 * ATTN_KV_WIDTH
PEER_HEADS = 8
PEER_N_KEYS = 128
PEER_HALF = 128
PEER_TOPK = 16
RMS_EPS = 1e-6
LNX_EPS = 64e-5
NEG_INF = -1e30

LANES = 128
SUBLANES = 8
VMEM_LIMIT_BYTES = 56 * 1024 * 1024

RWKV_CHUNK = 64
HEAD_PAIR = 2 * HEAD_DIM

_HI = lax.Precision.HIGHEST
_NN = (((1,), (0,)), ((), ()))
_NT = (((1,), (1,)), ((), ()))
_TN = (((0,), (0,)), ((), ()))


def _split(a):
    hi = a.astype(jnp.bfloat16)
    lo = (a - hi.astype(jnp.float32)).astype(jnp.bfloat16)
    return hi, lo


def _dot32(a, b, dims=_NN):
    ah, al = _split(a)
    bh, bl = _split(b)
    d = lambda p, q: lax.dot_general(p, q, dims, preferred_element_type=jnp.float32)
    return d(ah, bh) + (d(ah, bl) + d(al, bh))


def _dot16(a, b, dims=_NN):
    return lax.dot_general(a.astype(jnp.bfloat16), b.astype(jnp.bfloat16), dims,
                           preferred_element_type=jnp.float32)


def _cparams(sem):
    return pltpu.CompilerParams(dimension_semantics=sem, vmem_limit_bytes=VMEM_LIMIT_BYTES)


def _inproj_kernel(x_ref, g_ref, w_ref, b_ref, rkv_ref, lora_ref, attn_ref):
    x = x_ref[...]
    ms = jnp.mean(x * x, axis=-1, keepdims=True)
    h = (x * lax.rsqrt(ms + RMS_EPS) * g_ref[...]).astype(jnp.bfloat16)
    p = jnp.dot(h, w_ref[...], preferred_element_type=jnp.float32)
    rkv_ref[...] = p[:, :RKV_COLS]
    lora_ref[...] = p[:, RKV_COLS:RKV_COLS + LORA_PAD]
    attn_ref[...] = p[:, RKV_COLS + LORA_PAD:] + b_ref[...]


def _inproj(x2, ln1_g, w_in, b_attn, tm):
    T = x2.shape[0]
    w_rkv = w_in[:, :RKV_COLS]
    w_lora = jnp.pad(w_in[:, RKV_COLS:RKV_COLS + LORA_COLS], ((0, 0), (0, LORA_PAD - LORA_COLS)))
    w_attn = w_in[:, RKV_COLS + LORA_COLS:]
    w_all = jnp.concatenate([w_rkv, w_lora, w_attn], axis=1).astype(jnp.bfloat16)
    ncols = w_all.shape[1]
    return pl.pallas_call(
        _inproj_kernel,
        grid=(T // tm,),
        in_specs=[
            pl.BlockSpec((tm, D_MODEL), lambda i: (i, 0)),
            pl.BlockSpec((1, D_MODEL), lambda i: (0, 0)),
            pl.BlockSpec((D_MODEL, ncols), lambda i: (0, 0)),
            pl.BlockSpec((1, ATTN_COLS), lambda i: (0, 0)),
        ],
        out_specs=[
            pl.BlockSpec((tm, RKV_COLS), lambda i: (i, 0)),
            pl.BlockSpec((tm, LORA_PAD), lambda i: (i, 0)),
            pl.BlockSpec((tm, ATTN_COLS), lambda i: (i, 0)),
        ],
        out_shape=[
            jax.ShapeDtypeStruct((T, RKV_COLS), jnp.float32),
            jax.ShapeDtypeStruct((T, LORA_PAD), jnp.float32),
            jax.ShapeDtypeStruct((T, ATTN_COLS), jnp.float32),
        ],
        compiler_params=_cparams(("parallel",)),
        name="inproj",
    )(x2, ln1_g.reshape(1, D_MODEL), w_all, b_attn.reshape(1, ATTN_COLS))


def _rwkv_kernel(r_ref, k_ref, v_ref, lora_ref, mur_ref, muk_ref, muv_ref, mul_ref,
                 w0_ref, wup_ref, a0_ref, aup_ref, gup_ref, kk_ref, ka_ref, rk_ref,
                 lnw_ref, lnb_ref, o_ref, s_scr, rkv_buf, lora_buf):
    C = RWKV_CHUNK
    P = HEAD_PAIR
    f32 = jnp.float32
    c = pl.program_id(1)

    @pl.when(c == 0)
    def _():
        s_scr[...] = jnp.zeros_like(s_scr)
        rkv_buf[...] = jnp.zeros_like(rkv_buf)
        lora_buf[...] = jnp.zeros_like(lora_buf)

    def shifted(buf, j, cur, mu):
        buf[j, SUBLANES:SUBLANES + C, :] = cur
        prev = buf[j, SUBLANES - 1:SUBLANES - 1 + C, :]
        buf[j, SUBLANES - 1:SUBLANES, :] = cur[C - 1:C, :]
        return cur + mu * (prev - cur)

    r = shifted(rkv_buf, 0, r_ref[0], mur_ref[...])
    k = shifted(rkv_buf, 1, k_ref[0], muk_ref[...])
    v = shifted(rkv_buf, 2, v_ref[0], muv_ref[...])
    xl = shifted(lora_buf, 0, lora_ref[0], mul_ref[...])

    w_raw = w0_ref[...] + _dot16(jnp.tanh(xl), wup_ref[...])
    z = -w_raw
    softplus = jnp.maximum(z, 0.0) + jnp.log(1.0 + jnp.exp(-jnp.abs(z)))
    lw = -jnp.exp(-softplus - 0.5)
    a = jax.nn.sigmoid(a0_ref[...] + _dot16(xl, aup_ref[...]))
    g = _dot32(jax.nn.sigmoid(xl), gup_ref[...])

    lane = lax.broadcasted_iota(jnp.int32, (1, P), 1)
    m0 = lane < HEAD_DIM
    rowp = lax.broadcasted_iota(jnp.int32, (P, P), 0)
    colp = lax.broadcasted_iota(jnp.int32, (P, P), 1)
    same_head = (rowp < HEAD_DIM) == (colp < HEAD_DIM)
    bd = jnp.where(same_head, 1.0, 0.0).astype(jnp.bfloat16)
    strict2 = same_head & ((rowp & (C - 1)) > (colp & (C - 1)))
    incl2 = same_head & ((rowp & (C - 1)) >= (colp & (C - 1)))
    ti = lax.broadcasted_iota(jnp.int32, (C, C), 0)
    tj = lax.broadcasted_iota(jnp.int32, (C, C), 1)
    tril = jnp.where(ti >= tj, 1.0, 0.0).astype(f32)
    head_sum = lambda t: jnp.concatenate(
        [_dot16(t[:, p * P:(p + 1) * P], bd) for p in range(t.shape[1] // P)], axis=1)

    kk = k * kk_ref[...]
    kk = kk / jnp.maximum(jnp.sqrt(head_sum(kk * kk)), 1e-12)
    k2 = k * (1.0 + (a - 1.0) * ka_ref[...])
    bonus = head_sum(r * k2 * rk_ref[...]) * v

    lc = _dot32(tril, lw)
    e_pos = jnp.exp(lc)
    e_neg = jnp.exp(-lc)
    rt_all = r * e_pos
    at_all = -kk * jnp.exp(lc - lw)
    bt_all = kk * a * e_neg
    kt_all = k2 * e_neg

    twice = lambda t: jnp.concatenate([t, t], axis=0)
    stack = lambda t: jnp.concatenate([jnp.where(m0, t, 0.0), jnp.where(m0, 0.0, t)], axis=0)
    pairs = range(r.shape[1] // P)
    cut = lambda t: [t[:, p * P:(p + 1) * P] for p in pairs]
    at, rt, bt, kt, vp = cut(at_all), cut(rt_all), cut(bt_all), cut(kt_all), cut(v)
    s0 = [s_scr[p] for p in pairs]
    bk = [jnp.concatenate([stack(bt[p]), stack(kt[p])], axis=0) for p in pairs]
    pa = [_dot16(stack(at[p]), bk[p], _NT) for p in pairs]
    m = [jnp.where(strict2, pa[p][:, :P], 0.0) for p in pairs]
    v2 = [twice(vp[p]) for p in pairs]
    x = [twice(_dot16(at[p], s0[p], _NT)) + _dot16(jnp.where(strict2, pa[p][:, P:], 0.0), v2[p])
         for p in pairs]
    for step in range(6):
        x = [x[p] + _dot16(m[p], x[p]) for p in pairs]
        if step < 5:
            m = [_dot16(m[p], m[p]) for p in pairs]
    pr = [_dot32(stack(rt[p]), bk[p], _NT) for p in pairs]
    nr = [jnp.concatenate([jnp.where(incl2, pr[p][:, :P], 0.0), jnp.where(incl2, pr[p][:, P:], 0.0)], axis=1)
          for p in pairs]
    y2 = [twice(_dot16(rt[p], s0[p], _NT)) + _dot32(nr[p], jnp.concatenate([x[p], v2[p]], axis=0))
          for p in pairs]
    u = [jnp.where(m0, x[p][:C], x[p][C:]) for p in pairs]
    ys = [jnp.where(m0, y2[p][:C], y2[p][C:]) for p in pairs]
    for p in pairs:
        ds = _dot32(jnp.concatenate([u[p], vp[p]], axis=0), jnp.concatenate([bt[p], kt[p]], axis=0), _TN)
        s_scr[p] = jnp.where(same_head, s0[p] + ds, 0.0) * e_pos[C - 1:C, p * P:(p + 1) * P]
    y = jnp.concatenate(ys, axis=1)

    mean = head_sum(y) * (1.0 / HEAD_DIM)
    yc = y - mean
    var = head_sum(yc * yc) * (1.0 / HEAD_DIM)
    yn = yc * lax.rsqrt(var + LNX_EPS) * lnw_ref[...] + lnb_ref[...]
    o_ref[0] = (yn + bonus) * g


def _rwkv(p_rkv, p_lora, mu_shift, w0, w_up, a0, a_up, g_up, k_k, k_a, r_k, lnx_w, lnx_b):
    B, S, _ = p_rkv.shape
    C = RWKV_CHUNK
    W = RWKV_WIDTH
    row = lambda t: t.reshape(1, -1)
    mu_r, mu_k, mu_v = (row(mu_shift[i * W:(i + 1) * W]) for i in range(3))
    mu_l = row(jnp.pad(mu_shift[3 * W:], (0, LORA_PAD - LORA_COLS)))
    wup_p = jnp.zeros((LORA_PAD, W), jnp.float32).at[:D_DECAY_LORA].set(w_up)
    aup_p = jnp.zeros((LORA_PAD, W), jnp.float32).at[D_DECAY_LORA:D_DECAY_LORA + D_AAA_LORA].set(a_up)
    gup_p = jnp.zeros((LORA_PAD, W), jnp.float32).at[D_DECAY_LORA + D_AAA_LORA:LORA_COLS].set(g_up)
    vec = lambda: pl.BlockSpec((1, W), lambda b, c: (0, 0))
    mat = lambda: pl.BlockSpec((LORA_PAD, W), lambda b, c: (0, 0))
    col = lambda j: pl.BlockSpec((1, C, W), lambda b, c, j=j: (b, c, j))
    return pl.pallas_call(
        _rwkv_kernel,
        grid=(B, S // C),
        in_specs=[
            col(0), col(1), col(2),
            pl.BlockSpec((1, C, LORA_PAD), lambda b, c: (b, c, 0)),
            vec(), vec(), vec(),
            pl.BlockSpec((1, LORA_PAD), lambda b, c: (0, 0)),
            vec(), mat(), vec(), mat(), mat(), vec(), vec(), vec(), vec(), vec(),
        ],
        out_specs=pl.BlockSpec((1, C, W), lambda b, c: (b, c, 0)),
        out_shape=jax.ShapeDtypeStruct((B, S, W), jnp.float32),
        scratch_shapes=[
            pltpu.VMEM((W // HEAD_PAIR, HEAD_PAIR, HEAD_PAIR), jnp.float32),
            pltpu.VMEM((3, SUBLANES + C, W), jnp.float32),
            pltpu.VMEM((1, SUBLANES + C, LORA_PAD), jnp.float32),
        ],
        compiler_params=_cparams(("parallel", "arbitrary")),
        name="rwkv7",
    )(p_rkv, p_rkv, p_rkv, p_lora, mu_r, mu_k, mu_v, mu_l,
      row(w0), wup_p, row(a0), aup_p, gup_p, row(k_k), row(k_a), row(r_k), row(lnx_w), row(lnx_b))


def _attn_kernel(sink_ref, q_ref, kp_ref, kc_ref, vp_ref, vc_ref, g_ref, o_ref):
    n = pl.program_id(1)
    Wn = WINDOW
    scale = 1.0 / np.sqrt(HEAD_DIM)
    q = q_ref[0]
    kext = jnp.concatenate([kp_ref[0], kc_ref[0]], axis=0)
    vext = jnp.concatenate([vp_ref[0], vc_ref[0]], axis=0)
    qi = lax.broadcasted_iota(jnp.int32, (Wn, 2 * Wn), 0)
    kj = lax.broadcasted_iota(jnp.int32, (Wn, 2 * Wn), 1)
    diff = qi + Wn - kj
    allowed = (diff >= 0) & (diff < Wn) & ((n > 0) | (kj >= Wn))
    outs = []
    for h in range(ATTN_Q_HEADS):
        kv = h // ATTN_GROUP
        qh = q[:, h * HEAD_DIM:(h + 1) * HEAD_DIM]
        kh = kext[:, kv * HEAD_DIM:(kv + 1) * HEAD_DIM]
        vh = vext[:, kv * HEAD_DIM:(kv + 1) * HEAD_DIM]
        s = _dot16(qh, kh, _NT) * scale
        s = jnp.where(allowed, s, NEG_INF)
        sink = sink_ref[h]
        m = jnp.maximum(jnp.max(s, axis=-1, keepdims=True), sink)
        p = jnp.exp(s - m)
        denom = jnp.sum(p, axis=-1, keepdims=True) + jnp.exp(sink - m)
        outs.append(_dot16(p, vh) / denom)
    o = jnp.concatenate(outs, axis=-1)
    ms = jnp.mean(o * o, axis=-1, keepdims=True)
    o_ref[0] = o * lax.rsqrt(ms + RMS_EPS) * g_ref[...]


def _attention(p_attn, sinks, norm_g):
    B, S, _ = p_attn.shape
    Wn = WINDOW
    nb = S // Wn
    kcol = ATTN_WIDTH // ATTN_KV_WIDTH
    prev = lambda b, n: (b, jnp.maximum(n - 1, 0))
    return pl.pallas_call(
        _attn_kernel,
        grid=(B, nb),
        in_specs=[
            pl.BlockSpec(memory_space=pltpu.SMEM),
            pl.BlockSpec((1, Wn, ATTN_WIDTH), lambda b, n: (b, n, 0)),
            pl.BlockSpec((1, Wn, ATTN_KV_WIDTH), lambda b, n: prev(b, n) + (kcol,)),
            pl.BlockSpec((1, Wn, ATTN_KV_WIDTH), lambda b, n: (b, n, kcol)),
            pl.BlockSpec((1, Wn, ATTN_KV_WIDTH), lambda b, n: prev(b, n) + (kcol + 1,)),
            pl.BlockSpec((1, Wn, ATTN_KV_WIDTH), lambda b, n: (b, n, kcol + 1)),
            pl.BlockSpec((1, ATTN_WIDTH), lambda b, n: (0, 0)),
        ],
        out_specs=pl.BlockSpec((1, Wn, ATTN_WIDTH), lambda b, n: (b, n, 0)),
        out_shape=jax.ShapeDtypeStruct((B, S, ATTN_WIDTH), jnp.float32),
        compiler_params=_cparams(("parallel", "parallel")),
        name="swa_attn",
    )(sinks, p_attn, p_attn, p_attn, p_attn, p_attn, norm_g.reshape(1, ATTN_WIDTH))


def _outproj_kernel(x_ref, yr_ref, ya_ref, wr_ref, wa_ref, g_ref, wq_ref, sk_ref, x1_ref, h2_ref, st_ref):
    x1 = (x_ref[...] + _dot16(yr_ref[...], wr_ref[...]) + _dot16(ya_ref[...], wa_ref[...]))
    x1_ref[...] = x1
    ms = jnp.mean(x1 * x1, axis=-1, keepdims=True)
    h2 = x1 * lax.rsqrt(ms + RMS_EPS) * g_ref[...]
    h2_ref[...] = h2
    q = _dot16(h2, wq_ref[...])
    for hc in range(2 * PEER_HEADS):
        st_ref[hc] = _dot16(sk_ref[hc], q[:, hc * PEER_HALF:(hc + 1) * PEER_HALF], _NT)


def _outproj(x2, y_rwkv, y_attn, w_out, ln2_g, peer_wq, peer_subkeys, tm):
    T = x2.shape[0]
    nq = peer_wq.shape[1]
    nhc = 2 * PEER_HEADS
    w_r = w_out[:RWKV_WIDTH].astype(jnp.bfloat16)
    w_a = w_out[RWKV_WIDTH:].astype(jnp.bfloat16)
    sk = peer_subkeys.reshape(nhc, PEER_N_KEYS, PEER_HALF).astype(jnp.bfloat16)
    full = lambda shape: pl.BlockSpec(shape, lambda i: (0,) * len(shape))
    return pl.pallas_call(
        _outproj_kernel,
        grid=(T // tm,),
        in_specs=[
            pl.BlockSpec((tm, D_MODEL), lambda i: (i, 0)),
            pl.BlockSpec((tm, RWKV_WIDTH), lambda i: (i, 0)),
            pl.BlockSpec((tm, ATTN_WIDTH), lambda i: (i, 0)),
            full((RWKV_WIDTH, D_MODEL)), full((ATTN_WIDTH, D_MODEL)), full((1, D_MODEL)),
            full((D_MODEL, nq)), full((nhc, PEER_N_KEYS, PEER_HALF)),
        ],
        out_specs=[
            pl.BlockSpec((tm, D_MODEL), lambda i: (i, 0)),
            pl.BlockSpec((tm, D_MODEL), lambda i: (i, 0)),
            pl.BlockSpec((nhc, PEER_N_KEYS, tm), lambda i: (0, 0, i)),
        ],
        out_shape=[
            jax.ShapeDtypeStruct((T, D_MODEL), jnp.float32),
            jax.ShapeDtypeStruct((T, D_MODEL), jnp.float32),
            jax.ShapeDtypeStruct((nhc, PEER_N_KEYS, T), jnp.float32),
        ],
        compiler_params=_cparams(("parallel",)),
        name="outproj_peerq",
    )(x2, y_rwkv, y_attn, w_r, w_a, ln2_g.reshape(1, D_MODEL), peer_wq.astype(jnp.bfloat16), sk)


def _top16(s, order, payload=None):
    n, w = s.shape
    K = PEER_TOPK
    payload = order if payload is None else payload
    out_row = lax.broadcasted_iota(jnp.int32, (K, w), 0)
    cur = s
    vals = jnp.zeros((K, w), jnp.float32)
    picks = jnp.zeros((K, w), jnp.float32)
    for i in range(K):
        m = jnp.max(cur, axis=0, keepdims=True)
        first = jnp.min(jnp.where(cur == m, order, np.float32(1e9)), axis=0, keepdims=True)
        hit = order == first
        pick = first if payload is order else jnp.max(jnp.where(hit, payload, -1.0), axis=0, keepdims=True)
        vals = jnp.where(out_row == i, m, vals)
        picks = jnp.where(out_row == i, pick, picks)
        cur = jnp.where(hit, -jnp.inf, cur)
    return vals, picks


def _staircase():
    K = PEER_TOPK
    pairs = [(a, b) for a in range(K) for b in range(K) if (a + 1) * (b + 1) <= K]
    n = -(-len(pairs) // SUBLANES) * SUBLANES
    sel = np.zeros((2, n, K), np.float32)
    pos = np.full((n, LANES), -1.0, np.float32)
    for r, (a, b) in enumerate(pairs):
        sel[0, r, a] = 1.0
        sel[1, r, b] = 1.0
        pos[r] = a * K + b
    return sel, pos


def _split3(x):
    h1 = x.astype(jnp.bfloat16)
    r1 = x - h1.astype(jnp.float32)
    h2 = r1.astype(jnp.bfloat16)
    h3 = (r1 - h2.astype(jnp.float32)).astype(jnp.bfloat16)
    return h1, h2, h3


def _pick_rows(sel, x):
    d = lambda t: jnp.dot(sel, t, preferred_element_type=jnp.float32)
    h1, h2, h3 = _split3(x)
    return (d(h1) + d(h2)) + d(h3)


def _topk_kernel(st_ref, sel_ref, pos_ref, idx_ref, gate_ref):
    K = PEER_TOPK
    ncol = st_ref.shape[2] // LANES
    key_order = lax.broadcasted_iota(jnp.int32, (PEER_N_KEYS, LANES), 0).astype(jnp.float32)
    sel_a = sel_ref[0]
    sel_b = sel_ref[1]
    pos = pos_ref[...]
    live = pos >= 0.0

    def per_col(j, _):
        col = pl.ds(pl.multiple_of(j * LANES, LANES), LANES)
        for h in range(PEER_HEADS):
            s1, i1 = _top16(st_ref[2 * h, :, col], key_order)
            s2, i2 = _top16(st_ref[2 * h + 1, :, col], key_order)
            cand_s = jnp.where(live, _pick_rows(sel_a, s1) + _pick_rows(sel_b, s2), -jnp.inf)
            pick = lambda sel, t: jnp.dot(sel, t.astype(jnp.bfloat16), preferred_element_type=jnp.float32)
            cand_i = pick(sel_a, i1) * np.float32(PEER_N_KEYS) + pick(sel_b, i2)
            best_s, best_i = _top16(cand_s, pos, cand_i)
            e = jnp.exp(best_s - best_s[0:1])
            gate = e / jnp.sum(e, axis=0, keepdims=True)
            idx_ref[h * K:(h + 1) * K, col] = best_i.astype(jnp.int32)
            gate_ref[h * K:(h + 1) * K, col] = gate
        return 0

    lax.fori_loop(0, ncol, per_col, 0)


def _topk(scores_t, tm):
    nhc, nk, T = scores_t.shape
    ne = PEER_HEADS * PEER_TOPK
    sel, pos = _staircase()
    return pl.pallas_call(
        _topk_kernel,
        grid=(T // tm,),
        in_specs=[pl.BlockSpec((nhc, nk, tm), lambda i: (0, 0, i)),
                  pl.BlockSpec(sel.shape, lambda i: (0, 0, 0)),
                  pl.BlockSpec(pos.shape, lambda i: (0, 0))],
        out_specs=[pl.BlockSpec((ne, tm), lambda i: (0, i)), pl.BlockSpec((ne, tm), lambda i: (0, i))],
        out_shape=[jax.ShapeDtypeStruct((ne, T), jnp.int32), jax.ShapeDtypeStruct((ne, T), jnp.float32)],
        compiler_params=_cparams(("parallel",)),
        name="peer_topk",
    )(scores_t, jnp.asarray(sel, jnp.bfloat16), jnp.asarray(pos))


PEER_TOKEN_TILE = 128
PEER_SELECTED = PEER_HEADS * PEER_TOPK
HALF_ROWS = D_MODEL // 2 // LANES
_HI_MASK = np.uint32(0xFFFF0000)


def _pack_table(tab):
    n, d = tab.shape
    b = lax.bitcast_convert_type(tab.astype(jnp.bfloat16), jnp.uint16).astype(jnp.uint32)
    w = (b[:, :d // 2] << 16) | b[:, d // 2:]
    return w.reshape(n, d // 2 // LANES, LANES)


def _unpack(w):
    hi = lax.bitcast_convert_type(w & _HI_MASK, jnp.float32)
    lo = lax.bitcast_convert_type(w << 16, jnp.float32)
    return hi, lo


def _peer_act_kernel(idx_ref, x_ref, gate_ref, tab_ref, c_ref, prod_scr, part_scr, acc_scr):
    TB = PEER_TOKEN_TILE
    NE = PEER_SELECTED
    lane = lax.broadcasted_iota(jnp.int32, (NE, TB), 1)
    acc_scr[...] = jnp.zeros_like(acc_scr)
    part_scr[...] = jnp.zeros_like(part_scr)

    def fold(t):
        act = jnp.sum(part_scr[...], axis=-1, keepdims=True)
        acc_scr[...] = jnp.where(lane == t, act, acc_scr[...])

    def per_token(t, _):
        fold(t - 1)
        xt = x_ref[t]
        xh = xt[:HALF_ROWS]
        xl = xt[HALF_ROWS:]
        for e in range(NE):
            hi, lo = _unpack(tab_ref[idx_ref[t, e]])
            prod_scr[e * HALF_ROWS:(e + 1) * HALF_ROWS, :] = hi * xh + lo * xl
        part = prod_scr[pl.ds(0, NE, stride=HALF_ROWS), :]
        for s in range(1, HALF_ROWS):
            part = part + prod_scr[pl.ds(s, NE, stride=HALF_ROWS), :]
        part_scr[...] = part
        return 0

    lax.fori_loop(0, TB, per_token, 0)
    fold(TB - 1)
    a = acc_scr[...]
    gelu = 0.5 * a * (1.0 + lax.erf(a * np.float32(1.0 / np.sqrt(2.0))))
    c_ref[...] = (gate_ref[...] * gelu).T


def _peer_act(idx, h2_3d, gate_t, u_packed):
    T, NE = idx.shape
    TB = PEER_TOKEN_TILE
    nrow = D_MODEL // LANES
    return pl.pallas_call(
        _peer_act_kernel,
        grid=(T // TB,),
        in_specs=[
            pl.BlockSpec((TB, NE), lambda i: (i, 0), memory_space=pltpu.SMEM),
            pl.BlockSpec((TB, nrow, LANES), lambda i: (i, 0, 0)),
            pl.BlockSpec((NE, TB), lambda i: (0, i)),
            pl.BlockSpec(u_packed.shape, lambda i: (0, 0, 0), pipeline_mode=pl.Buffered(1)),
        ],
        out_specs=pl.BlockSpec((TB, NE), lambda i: (i, 0)),
        out_shape=jax.ShapeDtypeStruct((T, NE), jnp.float32),
        scratch_shapes=[pltpu.VMEM((NE * HALF_ROWS, LANES), jnp.float32), pltpu.VMEM((NE, LANES), jnp.float32),
                        pltpu.VMEM((NE, TB), jnp.float32)],
        compiler_params=_cparams(("arbitrary",)),
        name="peer_act",
    )(idx, h2_3d, gate_t, u_packed)


GATHER_STRIDE = PEER_SELECTED + SUBLANES


def _peer_out_kernel(idx_ref, c_ref, x1_ref, g_ref, tab_ref, o_ref, gat_a, gat_b):
    TB = PEER_TOKEN_TILE
    NE = PEER_SELECTED
    GS = GATHER_STRIDE
    sub = lax.broadcasted_iota(jnp.int32, (SUBLANES, NE), 0)

    def gather(buf, t):
        for e in range(NE):
            buf[pl.ds(e, HALF_ROWS, stride=GS), :] = tab_ref[idx_ref[t, e]]

    def tile(buf):
        his, los = [], []
        for s in range(HALF_ROWS):
            hi, lo = _unpack(buf[s * GS:s * GS + NE, :])
            his.append(hi.astype(jnp.bfloat16))
            los.append(lo.astype(jnp.bfloat16))
        return jnp.concatenate(his + los, axis=1)

    def per_group(gi, _):
        base = pl.multiple_of(gi * SUBLANES, SUBLANES)
        c8 = c_ref[pl.ds(base, SUBLANES), :]
        y8 = jnp.zeros((SUBLANES, D_MODEL), jnp.float32)
        for tt in range(SUBLANES):
            buf = gat_a if tt % 2 == 0 else gat_b
            gather(buf, base + tt)
            ch, cl = _split(jnp.where(sub == tt, c8, 0.0))
            out = jnp.dot(jnp.concatenate([ch, cl], axis=0), tile(buf), preferred_element_type=jnp.float32)
            y8 = y8 + (out[:SUBLANES] + out[SUBLANES:])
        z = x1_ref[pl.ds(base, SUBLANES), :] + y8
        ms = jnp.mean(z * z, axis=-1, keepdims=True)
        o_ref[pl.ds(base, SUBLANES), :] = z * lax.rsqrt(ms + RMS_EPS) * g_ref[...]
        return 0

    lax.fori_loop(0, TB // SUBLANES, per_group, 0)


def _peer_out(idx, c, x1, lnf_g, v_packed, n_tokens):
    T, NE = idx.shape
    TB = PEER_TOKEN_TILE
    gat = pltpu.VMEM((HALF_ROWS * GATHER_STRIDE, LANES), jnp.uint32)
    return pl.pallas_call(
        _peer_out_kernel,
        grid=(n_tokens // TB,),
        in_specs=[
            pl.BlockSpec((TB, NE), lambda i: (i, 0), memory_space=pltpu.SMEM),
            pl.BlockSpec((TB, NE), lambda i: (i, 0)),
            pl.BlockSpec((TB, D_MODEL), lambda i: (i, 0)),
            pl.BlockSpec((1, D_MODEL), lambda i: (0, 0)),
            pl.BlockSpec(v_packed.shape, lambda i: (0, 0, 0), pipeline_mode=pl.Buffered(1)),
        ],
        out_specs=pl.BlockSpec((TB, D_MODEL), lambda i: (i, 0)),
        out_shape=jax.ShapeDtypeStruct((T, D_MODEL), jnp.float32),
        scratch_shapes=[gat, gat],
        compiler_params=_cparams(("arbitrary",)),
        name="peer_out",
    )(idx, c, x1, lnf_g.reshape(1, D_MODEL), v_packed)


SC_LANES = 16
SC_WORKERS = 32
SC_GATHER_ROWS = 32
SC_TOKEN_ALIGN = 256


def _sc_token_share(T):
    return (T * 9 // 25) // SC_TOKEN_ALIGN * SC_TOKEN_ALIGN


def _peer_out_sc(idx, c, tab):
    n, NE = idx.shape
    D = tab.shape[1]
    L = SC_LANES
    R = SC_GATHER_ROWS
    nrow = D // LANES
    per_w = n // SC_WORKERS
    mesh = plsc.VectorSubcoreMesh(core_axis_name="c", subcore_axis_name="s")

    @functools.partial(
        pl.kernel, mesh=mesh, compiler_params=pltpu.CompilerParams(needs_layout_passes=False),
        out_type=jax.ShapeDtypeStruct((n, nrow, LANES), jnp.float32),
        scratch_types=[
            pltpu.VMEM((NE,), jnp.int32),
            pltpu.VMEM((NE,), jnp.float32),
            pltpu.VMEM((2, R, nrow, LANES), jnp.float32),
            pltpu.VMEM((nrow, LANES), jnp.float32),
            pltpu.SemaphoreType.DMA((2,)),
        ],
    )
    def sc_kernel(idx_hbm, c_hbm, tab_hbm, out_hbm, idx_v, c_v, rows_v, y_v, sems):
        wid = lax.axis_index("s") * 2 + lax.axis_index("c")

        def gather(ci, b):
            return pltpu.make_async_copy(tab_hbm.at[idx_v.at[pl.ds(ci * R, R)]], rows_v.at[b], sems.at[b])

        @pl.loop(0, per_w)
        def _(i):
            t = wid * per_w + i
            pltpu.sync_copy(idx_hbm.at[t], idx_v)
            pltpu.sync_copy(c_hbm.at[t], c_v)
            gather(0, 0).start()
            for ci in range(NE // R):
                b = ci % 2
                if ci + 1 < NE // R:
                    gather(ci + 1, 1 - b).start()
                gather(ci, b).wait()
                for g in range(nrow):
                    cols = [pl.ds(j * L, L) for j in range(LANES // L)]
                    if ci == 0:
                        accs = tuple(jnp.zeros((L,), jnp.float32) for _ in cols)
                    else:
                        accs = tuple(y_v[g, cs] for cs in cols)

                    def row_body(e, accs, b=b, ci=ci, cols=cols, g=g):
                        ce = plsc.load_gather(c_v, [lax.iota(jnp.int32, L) * 0 + (ci * R + e)])
                        return tuple(a + ce * rows_v[b, e, g, cs] for a, cs in zip(accs, cols))

                    accs = lax.fori_loop(0, R, row_body, accs)
                    for a, cs in zip(accs, cols):
                        y_v[g, cs] = a
            pltpu.sync_copy(y_v, out_hbm.at[t])

    return sc_kernel(idx, c, tab.reshape(tab.shape[0], nrow, LANES)).reshape(n, D)


def _tail_norm_kernel(buf_ref, x1_ref, y_ref, g_ref, o_ref):
    del buf_ref
    z = x1_ref[...] + y_ref[...]
    ms = jnp.mean(z * z, axis=-1, keepdims=True)
    o_ref[...] = z * lax.rsqrt(ms + RMS_EPS) * g_ref[...]


def _tail_norm(out_buf, x1, y_tail, lnf_g, tm):
    T, D = x1.shape
    n = y_tail.shape[0]
    first = (T - n) // tm
    return pl.pallas_call(
        _tail_norm_kernel,
        grid=(n // tm,),
        in_specs=[
            pl.BlockSpec(memory_space=pl.ANY),
            pl.BlockSpec((tm, D), lambda i: (first + i, 0)),
            pl.BlockSpec((tm, D), lambda i: (i, 0)),
            pl.BlockSpec((1, D), lambda i: (0, 0)),
        ],
        out_specs=pl.BlockSpec((tm, D), lambda i: (first + i, 0)),
        out_shape=jax.ShapeDtypeStruct((T, D), jnp.float32),
        input_output_aliases={0: 0},
        compiler_params=_cparams(("parallel",)),
        name="peer_tail_norm",
    )(out_buf, x1, y_tail, lnf_g.reshape(1, D))


def kernel(x, ln1_g, w_in, b_attn, mu_shift, w0, w_up, a0, a_up, g_up, k_k, k_a, r_k, lnx_w, lnx_b, attn_sinks, attn_norm_g, w_out, ln2_g, peer_wq, peer_subkeys, peer_u, peer_v, lnf_g):
    B, S, D = x.shape
    T = B * S
    x2 = x.reshape(T, D)
    p_rkv, p_lora, p_attn = _inproj(x2, ln1_g[0], w_in[0], b_attn[0], tm=512)
    y_rwkv = _rwkv(p_rkv.reshape(B, S, -1), p_lora.reshape(B, S, -1), mu_shift[0], w0[0], w_up[0], a0[0],
                   a_up[0], g_up[0], k_k[0], k_a[0], r_k[0].reshape(-1), lnx_w[0], lnx_b[0])
    y_attn = _attention(p_attn.reshape(B, S, -1), attn_sinks[0], attn_norm_g[0])
    x1, h2, scores_t = _outproj(x2, y_rwkv.reshape(T, -1), y_attn.reshape(T, -1), w_out[0], ln2_g[0], peer_wq[0],
                                peer_subkeys[0], tm=512)
    idx_t, gate_t = _topk(scores_t, tm=512)
    nrow = D // LANES
    idx = idx_t.T
    c = _peer_act(idx, h2.reshape(T, nrow, LANES), gate_t, _pack_table(peer_u[0]))
    n_sc = _sc_token_share(T)
    out = _peer_out(idx, c, x1, lnf_g, _pack_table(peer_v[0]), T - n_sc)
    if n_sc:
        y_tail = _peer_out_sc(idx[T - n_sc:], c[T - n_sc:], peer_v[0])
        out = _tail_norm(out, x1, y_tail, lnf_g, tm=SC_TOKEN_ALIGN)
    return out.reshape(B, S, D)
```

```python
import functools

import jax
import jax.numpy as jnp
import numpy as np
from jax import lax
from jax.experimental import pallas as pl
from jax.experimental.pallas import tpu as pltpu
from jax.experimental.pallas import tpu_sc as plsc

D_MODEL = 1024
HEAD_DIM = 64
RWKV_WIDTH = 512
RWKV_HEADS = 8
ATTN_WIDTH = 512
ATTN_Q_HEADS = 8
ATTN_KV_HEADS = 2
ATTN_GROUP = 4
ATTN_KV_WIDTH = 128
WINDOW = 128
D_DECAY_LORA = 32
D_AAA_LORA = 32
D_GATE_LORA = 96
LORA_COLS = D_DECAY_LORA + D_AAA_LORA + D_GATE_LORA
LORA_PAD = 256
RKV_COLS = 3 * RWKV_WIDTH
ATTN_COLS = ATTN_WIDTH + 2 * ATTN_KV_WIDTH
PEER_HEADS = 8
PEER_N_KEYS = 128
PEER_HALF = 128
PEER_TOPK = 16
RMS_EPS = 1e-6
LNX_EPS = 64e-5
NEG_INF = -1e30

LANES = 128
SUBLANES = 8
VMEM_LIMIT_BYTES = 56 * 1024 * 1024

RWKV_CHUNK = 64
HEAD_PAIR = 2 * HEAD_DIM

_HI = lax.Precision.HIGHEST
_NN = (((1,), (0,)), ((), ()))
_NT = (((1,), (1,)), ((), ()))
_TN = (((0,), (0,)), ((), ()))


def _split(a):
    hi = a.astype(jnp.bfloat16)
    lo = (a - hi.astype(jnp.float32)).astype(jnp.bfloat16)
    return hi, lo


def _dot32(a, b, dims=_NN):
    ah, al = _split(a)
    bh, bl = _split(b)
    d = lambda p, q: lax.dot_general(p, q, dims, preferred_element_type=jnp.float32)
    return d(ah, bh) + (d(ah, bl) + d(al, bh))


def _dot16(a, b, dims=_NN):
    return lax.dot_general(a.astype(jnp.bfloat16), b.astype(jnp.bfloat16), dims,
                           preferred_element_type=jnp.float32)


def _cparams(sem):
    return pltpu.CompilerParams(dimension_semantics=sem, vmem_limit_bytes=VMEM_LIMIT_BYTES)


def _inproj_kernel(x_ref, g_ref, w_ref, b_ref, rkv_ref, lora_ref, attn_ref):
    x = x_ref[...]
    ms = jnp.mean(x * x, axis=-1, keepdims=True)
    h = (x * lax.rsqrt(ms + RMS_EPS) * g_ref[...]).astype(jnp.bfloat16)
    p = jnp.dot(h, w_ref[...], preferred_element_type=jnp.float32)
    rkv_ref[...] = p[:, :RKV_COLS]
    lora_ref[...] = p[:, RKV_COLS:RKV_COLS + LORA_PAD]
    attn_ref[...] = p[:, RKV_COLS + LORA_PAD:] + b_ref[...]


def _inproj(x2, ln1_g, w_in, b_attn, tm):
    T = x2.shape[0]
    w_rkv = w_in[:, :RKV_COLS]
    w_lora = jnp.pad(w_in[:, RKV_COLS:RKV_COLS + LORA_COLS], ((0, 0), (0, LORA_PAD - LORA_COLS)))
    w_attn = w_in[:, RKV_COLS + LORA_COLS:]
    w_all = jnp.concatenate([w_rkv, w_lora, w_attn], axis=1).astype(jnp.bfloat16)
    ncols = w_all.shape[1]
    return pl.pallas_call(
        _inproj_kernel,
        grid=(T // tm,),
        in_specs=[
            pl.BlockSpec((tm, D_MODEL), lambda i: (i, 0)),
            pl.BlockSpec((1, D_MODEL), lambda i: (0, 0)),
            pl.BlockSpec((D_MODEL, ncols), lambda i: (0, 0)),
            pl.BlockSpec((1, ATTN_COLS), lambda i: (0, 0)),
        ],
        out_specs=[
            pl.BlockSpec((tm, RKV_COLS), lambda i: (i, 0)),
            pl.BlockSpec((tm, LORA_PAD), lambda i: (i, 0)),
            pl.BlockSpec((tm, ATTN_COLS), lambda i: (i, 0)),
        ],
        out_shape=[
            jax.ShapeDtypeStruct((T, RKV_COLS), jnp.float32),
            jax.ShapeDtypeStruct((T, LORA_PAD), jnp.float32),
            jax.ShapeDtypeStruct((T, ATTN_COLS), jnp.float32),
        ],
        compiler_params=_cparams(("parallel",)),
        name="inproj",
    )(x2, ln1_g.reshape(1, D_MODEL), w_all, b_attn.reshape(1, ATTN_COLS))


def _rwkv_kernel(r_ref, k_ref, v_ref, lora_ref, mur_ref, muk_ref, muv_ref, mul_ref,
                 w0_ref, wup_ref, a0_ref, aup_ref, gup_ref, kk_ref, ka_ref, rk_ref,
                 lnw_ref, lnb_ref, o_ref, s_scr, rkv_buf, lora_buf):
    C = RWKV_CHUNK
    P = HEAD_PAIR
    f32 = jnp.float32
    c = pl.program_id(1)

    @pl.when(c == 0)
    def _():
        s_scr[...] = jnp.zeros_like(s_scr)
        rkv_buf[...] = jnp.zeros_like(rkv_buf)
        lora_buf[...] = jnp.zeros_like(lora_buf)

    def shifted(buf, j, cur, mu):
        buf[j, SUBLANES:SUBLANES + C, :] = cur
        prev = buf[j, SUBLANES - 1:SUBLANES - 1 + C, :]
        buf[j, SUBLANES - 1:SUBLANES, :] = cur[C - 1:C, :]
        return cur + mu * (prev - cur)

    r = shifted(rkv_buf, 0, r_ref[0], mur_ref[...])
    k = shifted(rkv_buf, 1, k_ref[0], muk_ref[...])
    v = shifted(rkv_buf, 2, v_ref[0], muv_ref[...])
    xl = shifted(lora_buf, 0, lora_ref[0], mul_ref[...])

    w_raw = w0_ref[...] + _dot16(jnp.tanh(xl), wup_ref[...])
    z = -w_raw
    softplus = jnp.maximum(z, 0.0) + jnp.log(1.0 + jnp.exp(-jnp.abs(z)))
    lw = -jnp.exp(-softplus - 0.5)
    a = jax.nn.sigmoid(a0_ref[...] + _dot16(xl, aup_ref[...]))
    g = _dot32(jax.nn.sigmoid(xl), gup_ref[...])

    lane = lax.broadcasted_iota(jnp.int32, (1, P), 1)
    m0 = lane < HEAD_DIM
    rowp = lax.broadcasted_iota(jnp.int32, (P, P), 0)
    colp = lax.broadcasted_iota(jnp.int32, (P, P), 1)
    same_head = (rowp < HEAD_DIM) == (colp < HEAD_DIM)
    bd = jnp.where(same_head, 1.0, 0.0).astype(jnp.bfloat16)
    strict2 = same_head & ((rowp & (C - 1)) > (colp & (C - 1)))
    incl2 = same_head & ((rowp & (C - 1)) >= (colp & (C - 1)))
    ti = lax.broadcasted_iota(jnp.int32, (C, C), 0)
    tj = lax.broadcasted_iota(jnp.int32, (C, C), 1)
    tril = jnp.where(ti >= tj, 1.0, 0.0).astype(f32)
    head_sum = lambda t: jnp.concatenate(
        [_dot16(t[:, p * P:(p + 1) * P], bd) for p in range(t.shape[1] // P)], axis=1)

    kk = k * kk_ref[...]
    kk = kk / jnp.maximum(jnp.sqrt(head_sum(kk * kk)), 1e-12)
    k2 = k * (1.0 + (a - 1.0) * ka_ref[...])
    bonus = head_sum(r * k2 * rk_ref[...]) * v

    lc = _dot32(tril, lw)
    e_pos = jnp.exp(lc)
    e_neg = jnp.exp(-lc)
    rt_all = r * e_pos
    at_all = -kk * jnp.exp(lc - lw)
    bt_all = kk * a * e_neg
    kt_all = k2 * e_neg

    twice = lambda t: jnp.concatenate([t, t], axis=0)
    stack = lambda t: jnp.concatenate([jnp.where(m0, t, 0.0), jnp.where(m0, 0.0, t)], axis=0)
    pairs = range(r.shape[1] // P)
    cut = lambda t: [t[:, p * P:(p + 1) * P] for p in pairs]
    at, rt, bt, kt, vp = cut(at_all), cut(rt_all), cut(bt_all), cut(kt_all), cut(v)
    s0 = [s_scr[p] for p in pairs]
    bk = [jnp.concatenate([stack(bt[p]), stack(kt[p])], axis=0) for p in pairs]
    pa = [_dot16(stack(at[p]), bk[p], _NT) for p in pairs]
    m = [jnp.where(strict2, pa[p][:, :P], 0.0) for p in pairs]
    v2 = [twice(vp[p]) for p in pairs]
    x = [twice(_dot16(at[p], s0[p], _NT)) + _dot16(jnp.where(strict2, pa[p][:, P:], 0.0), v2[p])
         for p in pairs]
    for step in range(6):
        x = [x[p] + _dot16(m[p], x[p]) for p in pairs]
        if step < 5:
            m = [_dot16(m[p], m[p]) for p in pairs]
    pr = [_dot32(stack(rt[p]), bk[p], _NT) for p in pairs]
    nr = [jnp.concatenate([jnp.where(incl2, pr[p][:, :P], 0.0), jnp.where(incl2, pr[p][:, P:], 0.0)], axis=1)
          for p in pairs]
    y2 = [twice(_dot16(rt[p], s0[p], _NT)) + _dot32(nr[p], jnp.concatenate([x[p], v2[p]], axis=0))
          for p in pairs]
    u = [jnp.where(m0, x[p][:C], x[p][C:]) for p in pairs]
    ys = [jnp.where(m0, y2[p][:C], y2[p][C:]) for p in pairs]
    for p in pairs:
        ds = _dot32(jnp.concatenate([u[p], vp[p]], axis=0), jnp.concatenate([bt[p], kt[p]], axis=0), _TN)
        s_scr[p] = jnp.where(same_head, s0[p] + ds, 0.0) * e_pos[C - 1:C, p * P:(p + 1) * P]
    y = jnp.concatenate(ys, axis=1)

    mean = head_sum(y) * (1.0 / HEAD_DIM)
    yc = y - mean
    var = head_sum(yc * yc) * (1.0 / HEAD_DIM)
    yn = yc * lax.rsqrt(var + LNX_EPS) * lnw_ref[...] + lnb_ref[...]
    o_ref[0] = (yn + bonus) * g


def _rwkv(p_rkv, p_lora, mu_shift, w0, w_up, a0, a_up, g_up, k_k, k_a, r_k, lnx_w, lnx_b):
    B, S, _ = p_rkv.shape
    C = RWKV_CHUNK
    W = RWKV_WIDTH
    row = lambda t: t.reshape(1, -1)
    mu_r, mu_k, mu_v = (row(mu_shift[i * W:(i + 1) * W]) for i in range(3))
    mu_l = row(jnp.pad(mu_shift[3 * W:], (0, LORA_PAD - LORA_COLS)))
    wup_p = jnp.zeros((LORA_PAD, W), jnp.float32).at[:D_DECAY_LORA].set(w_up)
    aup_p = jnp.zeros((LORA_PAD, W), jnp.float32).at[D_DECAY_LORA:D_DECAY_LORA + D_AAA_LORA].set(a_up)
    gup_p = jnp.zeros((LORA_PAD, W), jnp.float32).at[D_DECAY_LORA + D_AAA_LORA:LORA_COLS].set(g_up)
    vec = lambda: pl.BlockSpec((1, W), lambda b, c: (0, 0))
    mat = lambda: pl.BlockSpec((LORA_PAD, W), lambda b, c: (0, 0))
    col = lambda j: pl.BlockSpec((1, C, W), lambda b, c, j=j: (b, c, j))
    return pl.pallas_call(
        _rwkv_kernel,
        grid=(B, S // C),
        in_specs=[
            col(0), col(1), col(2),
            pl.BlockSpec((1, C, LORA_PAD), lambda b, c: (b, c, 0)),
            vec(), vec(), vec(),
            pl.BlockSpec((1, LORA_PAD), lambda b, c: (0, 0)),
            vec(), mat(), vec(), mat(), mat(), vec(), vec(), vec(), vec(), vec(),
        ],
        out_specs=pl.BlockSpec((1, C, W), lambda b, c: (b, c, 0)),
        out_shape=jax.ShapeDtypeStruct((B, S, W), jnp.float32),
        scratch_shapes=[
            pltpu.VMEM((W // HEAD_PAIR, HEAD_PAIR, HEAD_PAIR), jnp.float32),
            pltpu.VMEM((3, SUBLANES + C, W), jnp.float32),
            pltpu.VMEM((1, SUBLANES + C, LORA_PAD), jnp.float32),
        ],
        compiler_params=_cparams(("parallel", "arbitrary")),
        name="rwkv7",
    )(p_rkv, p_rkv, p_rkv, p_lora, mu_r, mu_k, mu_v, mu_l,
      row(w0), wup_p, row(a0), aup_p, gup_p, row(k_k), row(k_a), row(r_k), row(lnx_w), row(lnx_b))


def _attn_kernel(sink_ref, q_ref, kp_ref, kc_ref, vp_ref, vc_ref, g_ref, o_ref):
    n = pl.program_id(1)
    Wn = WINDOW
    scale = 1.0 / np.sqrt(HEAD_DIM)
    q = q_ref[0]
    kext = jnp.concatenate([kp_ref[0], kc_ref[0]], axis=0)
    vext = jnp.concatenate([vp_ref[0], vc_ref[0]], axis=0)
    qi = lax.broadcasted_iota(jnp.int32, (Wn, 2 * Wn), 0)
    kj = lax.broadcasted_iota(jnp.int32, (Wn, 2 * Wn), 1)
    diff = qi + Wn - kj
    allowed = (diff >= 0) & (diff < Wn) & ((n > 0) | (kj >= Wn))
    outs = []
    for h in range(ATTN_Q_HEADS):
        kv = h // ATTN_GROUP
        qh = q[:, h * HEAD_DIM:(h + 1) * HEAD_DIM]
        kh = kext[:, kv * HEAD_DIM:(kv + 1) * HEAD_DIM]
        vh = vext[:, kv * HEAD_DIM:(kv + 1) * HEAD_DIM]
        s = _dot16(qh, kh, _NT) * scale
        s = jnp.where(allowed, s, NEG_INF)
        sink = sink_ref[h]
        m = jnp.maximum(jnp.max(s, axis=-1, keepdims=True), sink)
        p = jnp.exp(s - m)
        denom = jnp.sum(p, axis=-1, keepdims=True) + jnp.exp(sink - m)
        outs.append(_dot16(p, vh) / denom)
    o = jnp.concatenate(outs, axis=-1)
    ms = jnp.mean(o * o, axis=-1, keepdims=True)
    o_ref[0] = o * lax.rsqrt(ms + RMS_EPS) * g_ref[...]


def _attention(p_attn, sinks, norm_g):
    B, S, _ = p_attn.shape
    Wn = WINDOW
    nb = S // Wn
    kcol = ATTN_WIDTH // ATTN_KV_WIDTH
    prev = lambda b, n: (b, jnp.maximum(n - 1, 0))
    return pl.pallas_call(
        _attn_kernel,
        grid=(B, nb),
        in_specs=[
            pl.BlockSpec(memory_space=pltpu.SMEM),
            pl.BlockSpec((1, Wn, ATTN_WIDTH), lambda b, n: (b, n, 0)),
            pl.BlockSpec((1, Wn, ATTN_KV_WIDTH), lambda b, n: prev(b, n) + (kcol,)),
            pl.BlockSpec((1, Wn, ATTN_KV_WIDTH), lambda b, n: (b, n, kcol)),
            pl.BlockSpec((1, Wn, ATTN_KV_WIDTH), lambda b, n: prev(b, n) + (kcol + 1,)),
            pl.BlockSpec((1, Wn, ATTN_KV_WIDTH), lambda b, n: (b, n, kcol + 1)),
            pl.BlockSpec((1, ATTN_WIDTH), lambda b, n: (0, 0)),
        ],
        out_specs=pl.BlockSpec((1, Wn, ATTN_WIDTH), lambda b, n: (b, n, 0)),
        out_shape=jax.ShapeDtypeStruct((B, S, ATTN_WIDTH), jnp.float32),
        compiler_params=_cparams(("parallel", "parallel")),
        name="swa_attn",
    )(sinks, p_attn, p_attn, p_attn, p_attn, p_attn, norm_g.reshape(1, ATTN_WIDTH))


def _outproj_kernel(x_ref, yr_ref, ya_ref, wr_ref, wa_ref, g_ref, wq_ref, sk_ref, x1_ref, h2_ref, st_ref):
    x1 = (x_ref[...] + _dot16(yr_ref[...], wr_ref[...]) + _dot16(ya_ref[...], wa_ref[...]))
    x1_ref[...] = x1
    ms = jnp.mean(x1 * x1, axis=-1, keepdims=True)
    h2 = x1 * lax.rsqrt(ms + RMS_EPS) * g_ref[...]
    h2_ref[...] = h2
    q = _dot16(h2, wq_ref[...])
    for hc in range(2 * PEER_HEADS):
        st_ref[hc] = _dot16(sk_ref[hc], q[:, hc * PEER_HALF:(hc + 1) * PEER_HALF], _NT)


def _outproj(x2, y_rwkv, y_attn, w_out, ln2_g, peer_wq, peer_subkeys, tm):
    T = x2.shape[0]
    nq = peer_wq.shape[1]
    nhc = 2 * PEER_HEADS
    w_r = w_out[:RWKV_WIDTH].astype(jnp.bfloat16)
    w_a = w_out[RWKV_WIDTH:].astype(jnp.bfloat16)
    sk = peer_subkeys.reshape(nhc, PEER_N_KEYS, PEER_HALF).astype(jnp.bfloat16)
    full = lambda shape: pl.BlockSpec(shape, lambda i: (0,) * len(shape))
    return pl.pallas_call(
        _outproj_kernel,
        grid=(T // tm,),
        in_specs=[
            pl.BlockSpec((tm, D_MODEL), lambda i: (i, 0)),
            pl.BlockSpec((tm, RWKV_WIDTH), lambda i: (i, 0)),
            pl.BlockSpec((tm, ATTN_WIDTH), lambda i: (i, 0)),
            full((RWKV_WIDTH, D_MODEL)), full((ATTN_WIDTH, D_MODEL)), full((1, D_MODEL)),
            full((D_MODEL, nq)), full((nhc, PEER_N_KEYS, PEER_HALF)),
        ],
        out_specs=[
            pl.BlockSpec((tm, D_MODEL), lambda i: (i, 0)),
            pl.BlockSpec((tm, D_MODEL), lambda i: (i, 0)),
            pl.BlockSpec((nhc, PEER_N_KEYS, tm), lambda i: (0, 0, i)),
        ],
        out_shape=[
            jax.ShapeDtypeStruct((T, D_MODEL), jnp.float32),
            jax.ShapeDtypeStruct((T, D_MODEL), jnp.float32),
            jax.ShapeDtypeStruct((nhc, PEER_N_KEYS, T), jnp.float32),
        ],
        compiler_params=_cparams(("parallel",)),
        name="outproj_peerq",
    )(x2, y_rwkv, y_attn, w_r, w_a, ln2_g.reshape(1, D_MODEL), peer_wq.astype(jnp.bfloat16), sk)


def _top16(s, order, payload=None):
    n, w = s.shape
    K = PEER_TOPK
    payload = order if payload is None else payload
    out_row = lax.broadcasted_iota(jnp.int32, (K, w), 0)
    cur = s
    vals = jnp.zeros((K, w), jnp.float32)
    picks = jnp.zeros((K, w), jnp.float32)
    for i in range(K):
        m = jnp.max(cur, axis=0, keepdims=True)
        first = jnp.min(jnp.where(cur == m, order, np.float32(1e9)), axis=0, keepdims=True)
        hit = order == first
        pick = first if payload is order else jnp.max(jnp.where(hit, payload, -1.0), axis=0, keepdims=True)
        vals = jnp.where(out_row == i, m, vals)
        picks = jnp.where(out_row == i, pick, picks)
        cur = jnp.where(hit, -jnp.inf, cur)
    return vals, picks


def _staircase():
    K = PEER_TOPK
    pairs = [(a, b) for a in range(K) for b in range(K) if (a + 1) * (b + 1) <= K]
    n = -(-len(pairs) // SUBLANES) * SUBLANES
    sel = np.zeros((2, n, K), np.float32)
    pos = np.full((n, LANES), -1.0, np.float32)
    for r, (a, b) in enumerate(pairs):
        sel[0, r, a] = 1.0
        sel[1, r, b] = 1.0
        pos[r] = a * K + b
    return sel, pos


def _split3(x):
    h1 = x.astype(jnp.bfloat16)
    r1 = x - h1.astype(jnp.float32)
    h2 = r1.astype(jnp.bfloat16)
    h3 = (r1 - h2.astype(jnp.float32)).astype(jnp.bfloat16)
    return h1, h2, h3


def _pick_rows(sel, x):
    d = lambda t: jnp.dot(sel, t, preferred_element_type=jnp.float32)
    h1, h2, h3 = _split3(x)
    return (d(h1) + d(h2)) + d(h3)


def _topk_kernel(st_ref, sel_ref, pos_ref, idx_ref, gate_ref):
    K = PEER_TOPK
    ncol = st_ref.shape[2] // LANES
    key_order = lax.broadcasted_iota(jnp.int32, (PEER_N_KEYS, LANES), 0).astype(jnp.float32)
    sel_a = sel_ref[0]
    sel_b = sel_ref[1]
    pos = pos_ref[...]
    live = pos >= 0.0

    def per_col(j, _):
        col = pl.ds(pl.multiple_of(j * LANES, LANES), LANES)
        for h in range(PEER_HEADS):
            s1, i1 = _top16(st_ref[2 * h, :, col], key_order)
            s2, i2 = _top16(st_ref[2 * h + 1, :, col], key_order)
            cand_s = jnp.where(live, _pick_rows(sel_a, s1) + _pick_rows(sel_b, s2), -jnp.inf)
            pick = lambda sel, t: jnp.dot(sel, t.astype(jnp.bfloat16), preferred_element_type=jnp.float32)
            cand_i = pick(sel_a, i1) * np.float32(PEER_N_KEYS) + pick(sel_b, i2)
            best_s, best_i = _top16(cand_s, pos, cand_i)
            e = jnp.exp(best_s - best_s[0:1])
            gate = e / jnp.sum(e, axis=0, keepdims=True)
            idx_ref[h * K:(h + 1) * K, col] = best_i.astype(jnp.int32)
            gate_ref[h * K:(h + 1) * K, col] = gate
        return 0

    lax.fori_loop(0, ncol, per_col, 0)


def _topk(scores_t, tm):
    nhc, nk, T = scores_t.shape
    ne = PEER_HEADS * PEER_TOPK
    sel, pos = _staircase()
    return pl.pallas_call(
        _topk_kernel,
        grid=(T // tm,),
        in_specs=[pl.BlockSpec((nhc, nk, tm), lambda i: (0, 0, i)),
                  pl.BlockSpec(sel.shape, lambda i: (0, 0, 0)),
                  pl.BlockSpec(pos.shape, lambda i: (0, 0))],
        out_specs=[pl.BlockSpec((ne, tm), lambda i: (0, i)), pl.BlockSpec((ne, tm), lambda i: (0, i))],
        out_shape=[jax.ShapeDtypeStruct((ne, T), jnp.int32), jax.ShapeDtypeStruct((ne, T), jnp.float32)],
        compiler_params=_cparams(("parallel",)),
        name="peer_topk",
    )(scores_t, jnp.asarray(sel, jnp.bfloat16), jnp.asarray(pos))


PEER_TOKEN_TILE = 128
PEER_SELECTED = PEER_HEADS * PEER_TOPK
HALF_ROWS = D_MODEL // 2 // LANES
_HI_MASK = np.uint32(0xFFFF0000)


def _pack_table(tab):
    n, d = tab.shape
    b = lax.bitcast_convert_type(tab.astype(jnp.bfloat16), jnp.uint16).astype(jnp.uint32)
    w = (b[:, :d // 2] << 16) | b[:, d // 2:]
    return w.reshape(n, d // 2 // LANES, LANES)


def _unpack(w):
    hi = lax.bitcast_convert_type(w & _HI_MASK, jnp.float32)
    lo = lax.bitcast_convert_type(w << 16, jnp.float32)
    return hi, lo


def _peer_act_kernel(idx_ref, x_ref, gate_ref, tab_ref, c_ref, prod_scr, part_scr, acc_scr):
    TB = PEER_TOKEN_TILE
    NE = PEER_SELECTED
    lane = lax.broadcasted_iota(jnp.int32, (NE, TB), 1)
    acc_scr[...] = jnp.zeros_like(acc_scr)
    part_scr[...] = jnp.zeros_like(part_scr)

    def fold(t):
        act = jnp.sum(part_scr[...], axis=-1, keepdims=True)
        acc_scr[...] = jnp.where(lane == t, act, acc_scr[...])

    def per_token(t, _):
        fold(t - 1)
        xt = x_ref[t]
        xh = xt[:HALF_ROWS]
        xl = xt[HALF_ROWS:]
        for e in range(NE):
            hi, lo = _unpack(tab_ref[idx_ref[t, e]])
            prod_scr[e * HALF_ROWS:(e + 1) * HALF_ROWS, :] = hi * xh + lo * xl
        part = prod_scr[pl.ds(0, NE, stride=HALF_ROWS), :]
        for s in range(1, HALF_ROWS):
            part = part + prod_scr[pl.ds(s, NE, stride=HALF_ROWS), :]
        part_scr[...] = part
        return 0

    lax.fori_loop(0, TB, per_token, 0)
    fold(TB - 1)
    a = acc_scr[...]
    gelu = 0.5 * a * (1.0 + lax.erf(a * np.float32(1.0 / np.sqrt(2.0))))
    c_ref[...] = (gate_ref[...] * gelu).T


def _peer_act(idx, h2_3d, gate_t, u_packed, start, count):
    T, NE = idx.shape
    TB = PEER_TOKEN_TILE
    nrow = D_MODEL // LANES
    first = start // TB
    return pl.pallas_call(
        _peer_act_kernel,
        grid=(count // TB,),
        in_specs=[
            pl.BlockSpec((TB, NE), lambda i: (first + i, 0), memory_space=pltpu.SMEM),
            pl.BlockSpec((TB, nrow, LANES), lambda i: (first + i, 0, 0)),
            pl.BlockSpec((NE, TB), lambda i: (0, first + i)),
            pl.BlockSpec(u_packed.shape, lambda i: (0, 0, 0), pipeline_mode=pl.Buffered(1)),
        ],
        out_specs=pl.BlockSpec((TB, NE), lambda i: (i, 0)),
        out_shape=jax.ShapeDtypeStruct((count, NE), jnp.float32),
        scratch_shapes=[pltpu.VMEM((NE * HALF_ROWS, LANES), jnp.float32), pltpu.VMEM((NE, LANES), jnp.float32),
                        pltpu.VMEM((NE, TB), jnp.float32)],
        compiler_params=_cparams(("arbitrary",)),
        name="peer_act",
    )(idx, h2_3d, gate_t, u_packed)


GATHER_STRIDE = PEER_SELECTED + SUBLANES


def _peer_out_kernel(idx_ref, c_ref, x1_ref, g_ref, tab_ref, o_ref, gat_a, gat_b):
    TB = PEER_TOKEN_TILE
    NE = PEER_SELECTED
    GS = GATHER_STRIDE
    sub = lax.broadcasted_iota(jnp.int32, (SUBLANES, NE), 0)

    def gather(buf, t):
        for e in range(NE):
            buf[pl.ds(e, HALF_ROWS, stride=GS), :] = tab_ref[idx_ref[t, e]]

    def tile(buf):
        his, los = [], []
        for s in range(HALF_ROWS):
            hi, lo = _unpack(buf[s * GS:s * GS + NE, :])
            his.append(hi.astype(jnp.bfloat16))
            los.append(lo.astype(jnp.bfloat16))
        return jnp.concatenate(his + los, axis=1)

    def per_group(gi, _):
        base = pl.multiple_of(gi * SUBLANES, SUBLANES)
        c8 = c_ref[pl.ds(base, SUBLANES), :]
        y8 = jnp.zeros((SUBLANES, D_MODEL), jnp.float32)
        for tt in range(SUBLANES):
            buf = gat_a if tt % 2 == 0 else gat_b
            gather(buf, base + tt)
            ch, cl = _split(jnp.where(sub == tt, c8, 0.0))
            out = jnp.dot(jnp.concatenate([ch, cl], axis=0), tile(buf), preferred_element_type=jnp.float32)
            y8 = y8 + (out[:SUBLANES] + out[SUBLANES:])
        z = x1_ref[pl.ds(base, SUBLANES), :] + y8
        ms = jnp.mean(z * z, axis=-1, keepdims=True)
        o_ref[pl.ds(base, SUBLANES), :] = z * lax.rsqrt(ms + RMS_EPS) * g_ref[...]
        return 0

    lax.fori_loop(0, TB // SUBLANES, per_group, 0)


def _peer_out(idx, c, x1, lnf_g, v_packed, n_tokens):
    T, NE = idx.shape
    TB = PEER_TOKEN_TILE
    gat = pltpu.VMEM((HALF_ROWS * GATHER_STRIDE, LANES), jnp.uint32)
    return pl.pallas_call(
        _peer_out_kernel,
        grid=(n_tokens // TB,),
        in_specs=[
            pl.BlockSpec((TB, NE), lambda i: (i, 0), memory_space=pltpu.SMEM),
            pl.BlockSpec((TB, NE), lambda i: (i, 0)),
            pl.BlockSpec((TB, D_MODEL), lambda i: (i, 0)),
            pl.BlockSpec((1, D_MODEL), lambda i: (0, 0)),
            pl.BlockSpec(v_packed.shape, lambda i: (0, 0, 0), pipeline_mode=pl.Buffered(1)),
        ],
        out_specs=pl.BlockSpec((TB, D_MODEL), lambda i: (i, 0)),
        out_shape=jax.ShapeDtypeStruct((T, D_MODEL), jnp.float32),
        scratch_shapes=[gat, gat],
        compiler_params=_cparams(("arbitrary",)),
        name="peer_out",
    )(idx, c, x1, lnf_g.reshape(1, D_MODEL), v_packed)


SC_LANES = 16
SC_WORKERS = 32
SC_GATHER_ROWS = 32
SC_TOKEN_ALIGN = 256


def _sc_token_split(T):
    n_sc = (T * 3 // 5) // (2 * SC_TOKEN_ALIGN) * (2 * SC_TOKEN_ALIGN)
    n1 = (n_sc * 2 // 5) // SC_TOKEN_ALIGN * SC_TOKEN_ALIGN
    return n1, n_sc - n1


def _peer_out_sc(idx, c, tab):
    n, NE = idx.shape
    D = tab.shape[1]
    L = SC_LANES
    R = SC_GATHER_ROWS
    nrow = D // LANES
    per_w = n // SC_WORKERS
    mesh = plsc.VectorSubcoreMesh(core_axis_name="c", subcore_axis_name="s")

    @functools.partial(
        pl.kernel, mesh=mesh, compiler_params=pltpu.CompilerParams(needs_layout_passes=False),
        out_type=jax.ShapeDtypeStruct((n, nrow, LANES), jnp.float32),
        scratch_types=[
            pltpu.VMEM((NE,), jnp.int32),
            pltpu.VMEM((NE,), jnp.float32),
            pltpu.VMEM((2, R, nrow, LANES), jnp.float32),
            pltpu.VMEM((nrow, LANES), jnp.float32),
            pltpu.SemaphoreType.DMA((2,)),
        ],
    )
    def sc_kernel(idx_hbm, c_hbm, tab_hbm, out_hbm, idx_v, c_v, rows_v, y_v, sems):
        wid = lax.axis_index("s") * 2 + lax.axis_index("c")

        def gather(ci, b):
            return pltpu.make_async_copy(tab_hbm.at[idx_v.at[pl.ds(ci * R, R)]], rows_v.at[b], sems.at[b])

        @pl.loop(0, per_w)
        def _(i):
            t = wid * per_w + i
            pltpu.sync_copy(idx_hbm.at[t], idx_v)
            pltpu.sync_copy(c_hbm.at[t], c_v)
            gather(0, 0).start()
            for ci in range(NE // R):
                b = ci % 2
                if ci + 1 < NE // R:
                    gather(ci + 1, 1 - b).start()
                gather(ci, b).wait()
                for g in range(nrow):
                    cols = [pl.ds(j * L, L) for j in range(LANES // L)]
                    if ci == 0:
                        accs = tuple(jnp.zeros((L,), jnp.float32) for _ in cols)
                    else:
                        accs = tuple(y_v[g, cs] for cs in cols)

                    def row_body(e, accs, b=b, ci=ci, cols=cols, g=g):
                        ce = plsc.load_gather(c_v, [lax.iota(jnp.int32, L) * 0 + (ci * R + e)])
                        return tuple(a + ce * rows_v[b, e, g, cs] for a, cs in zip(accs, cols))

                    accs = lax.fori_loop(0, R, row_body, accs)
                    for a, cs in zip(accs, cols):
                        y_v[g, cs] = a
            pltpu.sync_copy(y_v, out_hbm.at[t])

    return sc_kernel(idx, c, tab.reshape(tab.shape[0], nrow, LANES)).reshape(n, D)


def _tail_norm_kernel(buf_ref, x1_ref, y_ref, g_ref, o_ref):
    del buf_ref
    z = x1_ref[...] + y_ref[...]
    ms = jnp.mean(z * z, axis=-1, keepdims=True)
    o_ref[...] = z * lax.rsqrt(ms + RMS_EPS) * g_ref[...]


def _tail_norm(out_buf, x1, y_tail, lnf_g, tm):
    T, D = x1.shape
    n = y_tail.shape[0]
    first = (T - n) // tm
    return pl.pallas_call(
        _tail_norm_kernel,
        grid=(n // tm,),
        in_specs=[
            pl.BlockSpec(memory_space=pl.ANY),
            pl.BlockSpec((tm, D), lambda i: (first + i, 0)),
            pl.BlockSpec((tm, D), lambda i: (i, 0)),
            pl.BlockSpec((1, D), lambda i: (0, 0)),
        ],
        out_specs=pl.BlockSpec((tm, D), lambda i: (first + i, 0)),
        out_shape=jax.ShapeDtypeStruct((T, D), jnp.float32),
        input_output_aliases={0: 0},
        compiler_params=_cparams(("parallel",)),
        name="peer_tail_norm",
    )(out_buf, x1, y_tail, lnf_g.reshape(1, D))


def kernel(x, ln1_g, w_in, b_attn, mu_shift, w0, w_up, a0, a_up, g_up, k_k, k_a, r_k, lnx_w, lnx_b, attn_sinks, attn_norm_g, w_out, ln2_g, peer_wq, peer_subkeys, peer_u, peer_v, lnf_g):
    B, S, D = x.shape
    T = B * S
    x2 = x.reshape(T, D)
    p_rkv, p_lora, p_attn = _inproj(x2, ln1_g[0], w_in[0], b_attn[0], tm=512)
    y_rwkv = _rwkv(p_rkv.reshape(B, S, -1), p_lora.reshape(B, S, -1), mu_shift[0], w0[0], w_up[0], a0[0],
                   a_up[0], g_up[0], k_k[0], k_a[0], r_k[0].reshape(-1), lnx_w[0], lnx_b[0])
    y_attn = _attention(p_attn.reshape(B, S, -1), attn_sinks[0], attn_norm_g[0])
    x1, h2, scores_t = _outproj(x2, y_rwkv.reshape(T, -1), y_attn.reshape(T, -1), w_out[0], ln2_g[0], peer_wq[0],
                                peer_subkeys[0], tm=512)
    idx_t, gate_t = _topk(scores_t, tm=512)
    nrow = D // LANES
    idx = idx_t.T
    h2_3d = h2.reshape(T, nrow, LANES)
    u_packed = _pack_table(peer_u[0])
    n1, n2 = _sc_token_split(T)
    n_tc = T - n1 - n2
    y_tail = []
    for start, count in ((n_tc, n1), (n_tc + n1, n2)):
        if count:
            c_g = _peer_act(idx, h2_3d, gate_t, u_packed, start, count)
            y_tail.append(_peer_out_sc(idx[start:start + count], c_g, peer_v[0]))
    c = _peer_act(idx, h2_3d, gate_t, u_packed, 0, n_tc)
    out = _peer_out(idx, c, x1, lnf_g, _pack_table(peer_v[0]), n_tc)
    if y_tail:
        out = _tail_norm(out, x1, jnp.concatenate(y_tail, axis=0), lnf_g, tm=SC_TOKEN_ALIGN)
    return out.reshape(B, S, D)
```

```python
import functools

import jax
import jax.numpy as jnp
import numpy as np
from jax import lax
from jax.experimental import pallas as pl
from jax.experimental.pallas import tpu as pltpu
from jax.experimental.pallas import tpu_sc as plsc

D_MODEL = 1024
HEAD_DIM = 64
RWKV_WIDTH = 512
RWKV_HEADS = 8
ATTN_WIDTH = 512
ATTN_Q_HEADS = 8
ATTN_KV_HEADS = 2
ATTN_GROUP = 4
ATTN_KV_WIDTH = 128
WINDOW = 128
D_DECAY_LORA = 32
D_AAA_LORA = 32
D_GATE_LORA = 96
LORA_COLS = D_DECAY_LORA + D_AAA_LORA + D_GATE_LORA
LORA_PAD = 256
RKV_COLS = 3 * RWKV_WIDTH
ATTN_COLS = ATTN_WIDTH + 2 * ATTN_KV_WIDTH
PEER_HEADS = 8
PEER_N_KEYS = 128
PEER_HALF = 128
PEER_TOPK = 16
RMS_EPS = 1e-6
LNX_EPS = 64e-5
NEG_INF = -1e30

LANES = 128
SUBLANES = 8
VMEM_LIMIT_BYTES = 56 * 1024 * 1024

RWKV_CHUNK = 64
HEAD_PAIR = 2 * HEAD_DIM

_HI = lax.Precision.HIGHEST
_NN = (((1,), (0,)), ((), ()))
_NT = (((1,), (1,)), ((), ()))
_TN = (((0,), (0,)), ((), ()))


def _split(a):
    hi = a.astype(jnp.bfloat16)
    lo = (a - hi.astype(jnp.float32)).astype(jnp.bfloat16)
    return hi, lo


def _dot32(a, b, dims=_NN):
    ah, al = _split(a)
    bh, bl = _split(b)
    d = lambda p, q: lax.dot_general(p, q, dims, preferred_element_type=jnp.float32)
    return d(ah, bh) + (d(ah, bl) + d(al, bh))


def _dot16(a, b, dims=_NN):
    return lax.dot_general(a.astype(jnp.bfloat16), b.astype(jnp.bfloat16), dims,
                           preferred_element_type=jnp.float32)


def _cparams(sem):
    return pltpu.CompilerParams(dimension_semantics=sem, vmem_limit_bytes=VMEM_LIMIT_BYTES)


def _inproj_kernel(x_ref, g_ref, w_ref, b_ref, rkv_ref, lora_ref, attn_ref):
    x = x_ref[...]
    ms = jnp.mean(x * x, axis=-1, keepdims=True)
    h = (x * lax.rsqrt(ms + RMS_EPS) * g_ref[...]).astype(jnp.bfloat16)
    p = jnp.dot(h, w_ref[...], preferred_element_type=jnp.float32)
    rkv_ref[...] = p[:, :RKV_COLS]
    lora_ref[...] = p[:, RKV_COLS:RKV_COLS + LORA_PAD]
    attn_ref[...] = p[:, RKV_COLS + LORA_PAD:] + b_ref[...]


def _inproj(x2, ln1_g, w_in, b_attn, tm, first_row, T):
    first = first_row // tm
    w_rkv = w_in[:, :RKV_COLS]
    w_lora = jnp.pad(w_in[:, RKV_COLS:RKV_COLS + LORA_COLS], ((0, 0), (0, LORA_PAD - LORA_COLS)))
    w_attn = w_in[:, RKV_COLS + LORA_COLS:]
    w_all = jnp.concatenate([w_rkv, w_lora, w_attn], axis=1).astype(jnp.bfloat16)
    ncols = w_all.shape[1]
    return pl.pallas_call(
        _inproj_kernel,
        grid=(T // tm,),
        in_specs=[
            pl.BlockSpec((tm, D_MODEL), lambda i: (first + i, 0)),
            pl.BlockSpec((1, D_MODEL), lambda i: (0, 0)),
            pl.BlockSpec((D_MODEL, ncols), lambda i: (0, 0)),
            pl.BlockSpec((1, ATTN_COLS), lambda i: (0, 0)),
        ],
        out_specs=[
            pl.BlockSpec((tm, RKV_COLS), lambda i: (i, 0)),
            pl.BlockSpec((tm, LORA_PAD), lambda i: (i, 0)),
            pl.BlockSpec((tm, ATTN_COLS), lambda i: (i, 0)),
        ],
        out_shape=[
            jax.ShapeDtypeStruct((T, RKV_COLS), jnp.float32),
            jax.ShapeDtypeStruct((T, LORA_PAD), jnp.float32),
            jax.ShapeDtypeStruct((T, ATTN_COLS), jnp.float32),
        ],
        compiler_params=_cparams(("parallel",)),
        name="inproj",
    )(x2, ln1_g.reshape(1, D_MODEL), w_all, b_attn.reshape(1, ATTN_COLS))


def _rwkv_kernel(r_ref, k_ref, v_ref, lora_ref, mur_ref, muk_ref, muv_ref, mul_ref,
                 w0_ref, wup_ref, a0_ref, aup_ref, gup_ref, kk_ref, ka_ref, rk_ref,
                 lnw_ref, lnb_ref, o_ref, s_scr, rkv_buf, lora_buf):
    C = RWKV_CHUNK
    P = HEAD_PAIR
    f32 = jnp.float32
    c = pl.program_id(1)

    @pl.when(c == 0)
    def _():
        s_scr[...] = jnp.zeros_like(s_scr)
        rkv_buf[...] = jnp.zeros_like(rkv_buf)
        lora_buf[...] = jnp.zeros_like(lora_buf)

    def shifted(buf, j, cur, mu):
        buf[j, SUBLANES:SUBLANES + C, :] = cur
        prev = buf[j, SUBLANES - 1:SUBLANES - 1 + C, :]
        buf[j, SUBLANES - 1:SUBLANES, :] = cur[C - 1:C, :]
        return cur + mu * (prev - cur)

    r = shifted(rkv_buf, 0, r_ref[0], mur_ref[...])
    k = shifted(rkv_buf, 1, k_ref[0], muk_ref[...])
    v = shifted(rkv_buf, 2, v_ref[0], muv_ref[...])
    xl = shifted(lora_buf, 0, lora_ref[0], mul_ref[...])

    w_raw = w0_ref[...] + _dot16(jnp.tanh(xl), wup_ref[...])
    z = -w_raw
    softplus = jnp.maximum(z, 0.0) + jnp.log(1.0 + jnp.exp(-jnp.abs(z)))
    lw = -jnp.exp(-softplus - 0.5)
    a = jax.nn.sigmoid(a0_ref[...] + _dot16(xl, aup_ref[...]))
    g = _dot32(jax.nn.sigmoid(xl), gup_ref[...])

    lane = lax.broadcasted_iota(jnp.int32, (1, P), 1)
    m0 = lane < HEAD_DIM
    rowp = lax.broadcasted_iota(jnp.int32, (P, P), 0)
    colp = lax.broadcasted_iota(jnp.int32, (P, P), 1)
    same_head = (rowp < HEAD_DIM) == (colp < HEAD_DIM)
    bd = jnp.where(same_head, 1.0, 0.0).astype(jnp.bfloat16)
    strict2 = same_head & ((rowp & (C - 1)) > (colp & (C - 1)))
    incl2 = same_head & ((rowp & (C - 1)) >= (colp & (C - 1)))
    ti = lax.broadcasted_iota(jnp.int32, (C, C), 0)
    tj = lax.broadcasted_iota(jnp.int32, (C, C), 1)
    tril = jnp.where(ti >= tj, 1.0, 0.0).astype(f32)
    head_sum = lambda t: jnp.concatenate(
        [_dot16(t[:, p * P:(p + 1) * P], bd) for p in range(t.shape[1] // P)], axis=1)

    kk = k * kk_ref[...]
    kk = kk / jnp.maximum(jnp.sqrt(head_sum(kk * kk)), 1e-12)
    k2 = k * (1.0 + (a - 1.0) * ka_ref[...])
    bonus = head_sum(r * k2 * rk_ref[...]) * v

    lc = _dot32(tril, lw)
    e_pos = jnp.exp(lc)
    e_neg = jnp.exp(-lc)
    rt_all = r * e_pos
    at_all = -kk * jnp.exp(lc - lw)
    bt_all = kk * a * e_neg
    kt_all = k2 * e_neg

    twice = lambda t: jnp.concatenate([t, t], axis=0)
    stack = lambda t: jnp.concatenate([jnp.where(m0, t, 0.0), jnp.where(m0, 0.0, t)], axis=0)
    pairs = range(r.shape[1] // P)
    cut = lambda t: [t[:, p * P:(p + 1) * P] for p in pairs]
    at, rt, bt, kt, vp = cut(at_all), cut(rt_all), cut(bt_all), cut(kt_all), cut(v)
    s0 = [s_scr[p] for p in pairs]
    bk = [jnp.concatenate([stack(bt[p]), stack(kt[p])], axis=0) for p in pairs]
    pa = [_dot16(stack(at[p]), bk[p], _NT) for p in pairs]
    m = [jnp.where(strict2, pa[p][:, :P], 0.0) for p in pairs]
    v2 = [twice(vp[p]) for p in pairs]
    x = [twice(_dot16(at[p], s0[p], _NT)) + _dot16(jnp.where(strict2, pa[p][:, P:], 0.0), v2[p])
         for p in pairs]
    for step in range(6):
        x = [x[p] + _dot16(m[p], x[p]) for p in pairs]
        if step < 5:
            m = [_dot16(m[p], m[p]) for p in pairs]
    pr = [_dot32(stack(rt[p]), bk[p], _NT) for p in pairs]
    nr = [jnp.concatenate([jnp.where(incl2, pr[p][:, :P], 0.0), jnp.where(incl2, pr[p][:, P:], 0.0)], axis=1)
          for p in pairs]
    y2 = [twice(_dot16(rt[p], s0[p], _NT)) + _dot32(nr[p], jnp.concatenate([x[p], v2[p]], axis=0))
          for p in pairs]
    u = [jnp.where(m0, x[p][:C], x[p][C:]) for p in pairs]
    ys = [jnp.where(m0, y2[p][:C], y2[p][C:]) for p in pairs]
    for p in pairs:
        ds = _dot32(jnp.concatenate([u[p], vp[p]], axis=0), jnp.concatenate([bt[p], kt[p]], axis=0), _TN)
        s_scr[p] = jnp.where(same_head, s0[p] + ds, 0.0) * e_pos[C - 1:C, p * P:(p + 1) * P]
    y = jnp.concatenate(ys, axis=1)

    mean = head_sum(y) * (1.0 / HEAD_DIM)
    yc = y - mean
    var = head_sum(yc * yc) * (1.0 / HEAD_DIM)
    yn = yc * lax.rsqrt(var + LNX_EPS) * lnw_ref[...] + lnb_ref[...]
    o_ref[0] = (yn + bonus) * g


def _rwkv(p_rkv, p_lora, mu_shift, w0, w_up, a0, a_up, g_up, k_k, k_a, r_k, lnx_w, lnx_b):
    B, S, _ = p_rkv.shape
    C = RWKV_CHUNK
    W = RWKV_WIDTH
    row = lambda t: t.reshape(1, -1)
    mu_r, mu_k, mu_v = (row(mu_shift[i * W:(i + 1) * W]) for i in range(3))
    mu_l = row(jnp.pad(mu_shift[3 * W:], (0, LORA_PAD - LORA_COLS)))
    wup_p = jnp.zeros((LORA_PAD, W), jnp.float32).at[:D_DECAY_LORA].set(w_up)
    aup_p = jnp.zeros((LORA_PAD, W), jnp.float32).at[D_DECAY_LORA:D_DECAY_LORA + D_AAA_LORA].set(a_up)
    gup_p = jnp.zeros((LORA_PAD, W), jnp.float32).at[D_DECAY_LORA + D_AAA_LORA:LORA_COLS].set(g_up)
    vec = lambda: pl.BlockSpec((1, W), lambda b, c: (0, 0))
    mat = lambda: pl.BlockSpec((LORA_PAD, W), lambda b, c: (0, 0))
    col = lambda j: pl.BlockSpec((1, C, W), lambda b, c, j=j: (b, c, j))
    return pl.pallas_call(
        _rwkv_kernel,
        grid=(B, S // C),
        in_specs=[
            col(0), col(1), col(2),
            pl.BlockSpec((1, C, LORA_PAD), lambda b, c: (b, c, 0)),
            vec(), vec(), vec(),
            pl.BlockSpec((1, LORA_PAD), lambda b, c: (0, 0)),
            vec(), mat(), vec(), mat(), mat(), vec(), vec(), vec(), vec(), vec(),
        ],
        out_specs=pl.BlockSpec((1, C, W), lambda b, c: (b, c, 0)),
        out_shape=jax.ShapeDtypeStruct((B, S, W), jnp.float32),
        scratch_shapes=[
            pltpu.VMEM((W // HEAD_PAIR, HEAD_PAIR, HEAD_PAIR), jnp.float32),
            pltpu.VMEM((3, SUBLANES + C, W), jnp.float32),
            pltpu.VMEM((1, SUBLANES + C, LORA_PAD), jnp.float32),
        ],
        compiler_params=_cparams(("parallel", "arbitrary")),
        name="rwkv7",
    )(p_rkv, p_rkv, p_rkv, p_lora, mu_r, mu_k, mu_v, mu_l,
      row(w0), wup_p, row(a0), aup_p, gup_p, row(k_k), row(k_a), row(r_k), row(lnx_w), row(lnx_b))


def _attn_kernel(sink_ref, q_ref, kp_ref, kc_ref, vp_ref, vc_ref, g_ref, o_ref):
    n = pl.program_id(1)
    Wn = WINDOW
    scale = 1.0 / np.sqrt(HEAD_DIM)
    q = q_ref[0]
    kext = jnp.concatenate([kp_ref[0], kc_ref[0]], axis=0)
    vext = jnp.concatenate([vp_ref[0], vc_ref[0]], axis=0)
    qi = lax.broadcasted_iota(jnp.int32, (Wn, 2 * Wn), 0)
    kj = lax.broadcasted_iota(jnp.int32, (Wn, 2 * Wn), 1)
    diff = qi + Wn - kj
    allowed = (diff >= 0) & (diff < Wn) & ((n > 0) | (kj >= Wn))
    outs = []
    for h in range(ATTN_Q_HEADS):
        kv = h // ATTN_GROUP
        qh = q[:, h * HEAD_DIM:(h + 1) * HEAD_DIM]
        kh = kext[:, kv * HEAD_DIM:(kv + 1) * HEAD_DIM]
        vh = vext[:, kv * HEAD_DIM:(kv + 1) * HEAD_DIM]
        s = _dot16(qh, kh, _NT) * scale
        s = jnp.where(allowed, s, NEG_INF)
        sink = sink_ref[h]
        m = jnp.maximum(jnp.max(s, axis=-1, keepdims=True), sink)
        p = jnp.exp(s - m)
        denom = jnp.sum(p, axis=-1, keepdims=True) + jnp.exp(sink - m)
        outs.append(_dot16(p, vh) / denom)
    o = jnp.concatenate(outs, axis=-1)
    ms = jnp.mean(o * o, axis=-1, keepdims=True)
    o_ref[0] = o * lax.rsqrt(ms + RMS_EPS) * g_ref[...]


def _attention(p_attn, sinks, norm_g):
    B, S, _ = p_attn.shape
    Wn = WINDOW
    nb = S // Wn
    kcol = ATTN_WIDTH // ATTN_KV_WIDTH
    prev = lambda b, n: (b, jnp.maximum(n - 1, 0))
    return pl.pallas_call(
        _attn_kernel,
        grid=(B, nb),
        in_specs=[
            pl.BlockSpec(memory_space=pltpu.SMEM),
            pl.BlockSpec((1, Wn, ATTN_WIDTH), lambda b, n: (b, n, 0)),
            pl.BlockSpec((1, Wn, ATTN_KV_WIDTH), lambda b, n: prev(b, n) + (kcol,)),
            pl.BlockSpec((1, Wn, ATTN_KV_WIDTH), lambda b, n: (b, n, kcol)),
            pl.BlockSpec((1, Wn, ATTN_KV_WIDTH), lambda b, n: prev(b, n) + (kcol + 1,)),
            pl.BlockSpec((1, Wn, ATTN_KV_WIDTH), lambda b, n: (b, n, kcol + 1)),
            pl.BlockSpec((1, ATTN_WIDTH), lambda b, n: (0, 0)),
        ],
        out_specs=pl.BlockSpec((1, Wn, ATTN_WIDTH), lambda b, n: (b, n, 0)),
        out_shape=jax.ShapeDtypeStruct((B, S, ATTN_WIDTH), jnp.float32),
        compiler_params=_cparams(("parallel", "parallel")),
        name="swa_attn",
    )(sinks, p_attn, p_attn, p_attn, p_attn, p_attn, norm_g.reshape(1, ATTN_WIDTH))


def _outproj_kernel(x_ref, yr_ref, ya_ref, wr_ref, wa_ref, g_ref, wq_ref, sk_ref, x1_ref, h2_ref, st_ref):
    x1 = (x_ref[...] + _dot16(yr_ref[...], wr_ref[...]) + _dot16(ya_ref[...], wa_ref[...]))
    x1_ref[...] = x1
    ms = jnp.mean(x1 * x1, axis=-1, keepdims=True)
    h2 = x1 * lax.rsqrt(ms + RMS_EPS) * g_ref[...]
    for j in range(D_MODEL // LANES):
        h2_ref[:, j, :] = h2[:, j * LANES:(j + 1) * LANES]
    q = _dot16(h2, wq_ref[...])
    for hc in range(2 * PEER_HEADS):
        st_ref[hc] = _dot16(sk_ref[hc], q[:, hc * PEER_HALF:(hc + 1) * PEER_HALF], _NT)


def _outproj(x2, y_rwkv, y_attn, w_out, ln2_g, peer_wq, peer_subkeys, tm, first_row):
    T = y_rwkv.shape[0]
    first = first_row // tm
    nq = peer_wq.shape[1]
    nhc = 2 * PEER_HEADS
    w_r = w_out[:RWKV_WIDTH].astype(jnp.bfloat16)
    w_a = w_out[RWKV_WIDTH:].astype(jnp.bfloat16)
    sk = peer_subkeys.reshape(nhc, PEER_N_KEYS, PEER_HALF).astype(jnp.bfloat16)
    full = lambda shape: pl.BlockSpec(shape, lambda i: (0,) * len(shape))
    return pl.pallas_call(
        _outproj_kernel,
        grid=(T // tm,),
        in_specs=[
            pl.BlockSpec((tm, D_MODEL), lambda i: (first + i, 0)),
            pl.BlockSpec((tm, RWKV_WIDTH), lambda i: (i, 0)),
            pl.BlockSpec((tm, ATTN_WIDTH), lambda i: (i, 0)),
            full((RWKV_WIDTH, D_MODEL)), full((ATTN_WIDTH, D_MODEL)), full((1, D_MODEL)),
            full((D_MODEL, nq)), full((nhc, PEER_N_KEYS, PEER_HALF)),
        ],
        out_specs=[
            pl.BlockSpec((tm, D_MODEL), lambda i: (i, 0)),
            pl.BlockSpec((tm, D_MODEL // LANES, LANES), lambda i: (i, 0, 0)),
            pl.BlockSpec((nhc, PEER_N_KEYS, tm), lambda i: (0, 0, i)),
        ],
        out_shape=[
            jax.ShapeDtypeStruct((T, D_MODEL), jnp.float32),
            jax.ShapeDtypeStruct((T, D_MODEL // LANES, LANES), jnp.float32),
            jax.ShapeDtypeStruct((nhc, PEER_N_KEYS, T), jnp.float32),
        ],
        compiler_params=_cparams(("parallel",)),
        name="outproj_peerq",
    )(x2, y_rwkv, y_attn, w_r, w_a, ln2_g.reshape(1, D_MODEL), peer_wq.astype(jnp.bfloat16), sk)


def _top16(s, order, payload=None):
    n, w = s.shape
    K = PEER_TOPK
    payload = order if payload is None else payload
    out_row = lax.broadcasted_iota(jnp.int32, (K, w), 0)
    cur = s
    vals = jnp.zeros((K, w), jnp.float32)
    picks = jnp.zeros((K, w), jnp.float32)
    for i in range(K):
        m = jnp.max(cur, axis=0, keepdims=True)
        first = jnp.min(jnp.where(cur == m, order, np.float32(1e9)), axis=0, keepdims=True)
        hit = order == first
        pick = first if payload is order else jnp.max(jnp.where(hit, payload, -1.0), axis=0, keepdims=True)
        vals = jnp.where(out_row == i, m, vals)
        picks = jnp.where(out_row == i, pick, picks)
        cur = jnp.where(hit, -jnp.inf, cur)
    return vals, picks


def _staircase():
    K = PEER_TOPK
    pairs = [(a, b) for a in range(K) for b in range(K) if (a + 1) * (b + 1) <= K]
    n = -(-len(pairs) // SUBLANES) * SUBLANES
    sel = np.zeros((2, n, K), np.float32)
    pos = np.full((n, LANES), -1.0, np.float32)
    for r, (a, b) in enumerate(pairs):
        sel[0, r, a] = 1.0
        sel[1, r, b] = 1.0
        pos[r] = a * K + b
    return sel, pos


def _split3(x):
    h1 = x.astype(jnp.bfloat16)
    r1 = x - h1.astype(jnp.float32)
    h2 = r1.astype(jnp.bfloat16)
    h3 = (r1 - h2.astype(jnp.float32)).astype(jnp.bfloat16)
    return h1, h2, h3


def _pick_rows(sel, x):
    d = lambda t: jnp.dot(sel, t, preferred_element_type=jnp.float32)
    h1, h2, h3 = _split3(x)
    return (d(h1) + d(h2)) + d(h3)


def _topk_kernel(st_ref, sel_ref, pos_ref, idx_ref, gate_ref):
    K = PEER_TOPK
    ncol = st_ref.shape[2] // LANES
    key_order = lax.broadcasted_iota(jnp.int32, (PEER_N_KEYS, LANES), 0).astype(jnp.float32)
    sel_a = sel_ref[0]
    sel_b = sel_ref[1]
    pos = pos_ref[...]
    live = pos >= 0.0

    def per_col(j, _):
        col = pl.ds(pl.multiple_of(j * LANES, LANES), LANES)
        picks = []
        for h in range(PEER_HEADS):
            s1, i1 = _top16(st_ref[2 * h, :, col], key_order)
            s2, i2 = _top16(st_ref[2 * h + 1, :, col], key_order)
            cand_s = jnp.where(live, _pick_rows(sel_a, s1) + _pick_rows(sel_b, s2), -jnp.inf)
            pick = lambda sel, t: jnp.dot(sel, t.astype(jnp.bfloat16), preferred_element_type=jnp.float32)
            cand_i = pick(sel_a, i1) * np.float32(PEER_N_KEYS) + pick(sel_b, i2)
            best_s, best_i = _top16(cand_s, pos, cand_i)
            e = jnp.exp(best_s - best_s[0:1])
            gate = e / jnp.sum(e, axis=0, keepdims=True)
            picks.append(best_i)
            gate_ref[h * K:(h + 1) * K, col] = gate
        idx_ref[col, :] = jnp.concatenate(picks, axis=0).T.astype(jnp.int32)
        return 0

    lax.fori_loop(0, ncol, per_col, 0)


def _topk(scores_t, tm):
    nhc, nk, T = scores_t.shape
    ne = PEER_HEADS * PEER_TOPK
    sel, pos = _staircase()
    return pl.pallas_call(
        _topk_kernel,
        grid=(T // tm,),
        in_specs=[pl.BlockSpec((nhc, nk, tm), lambda i: (0, 0, i)),
                  pl.BlockSpec(sel.shape, lambda i: (0, 0, 0)),
                  pl.BlockSpec(pos.shape, lambda i: (0, 0))],
        out_specs=[pl.BlockSpec((tm, ne), lambda i: (i, 0)), pl.BlockSpec((ne, tm), lambda i: (0, i))],
        out_shape=[jax.ShapeDtypeStruct((T, ne), jnp.int32), jax.ShapeDtypeStruct((ne, T), jnp.float32)],
        compiler_params=_cparams(("parallel",)),
        name="peer_topk",
    )(scores_t, jnp.asarray(sel, jnp.bfloat16), jnp.asarray(pos))


PEER_TOKEN_TILE = 128
PEER_SELECTED = PEER_HEADS * PEER_TOPK
HALF_ROWS = D_MODEL // 2 // LANES
_HI_MASK = np.uint32(0xFFFF0000)


def _pack_table(tab):
    n, d = tab.shape
    b = lax.bitcast_convert_type(tab.astype(jnp.bfloat16), jnp.uint16).astype(jnp.uint32)
    w = (b[:, :d // 2] << 16) | b[:, d // 2:]
    return w.reshape(n, d // 2 // LANES, LANES)


def _unpack(w):
    hi = lax.bitcast_convert_type(w & _HI_MASK, jnp.float32)
    lo = lax.bitcast_convert_type(w << 16, jnp.float32)
    return hi, lo


def _peer_act_kernel(idx_ref, x_ref, gate_ref, tab_ref, c_ref, prod_scr, part_scr, acc_scr):
    TB = PEER_TOKEN_TILE
    NE = PEER_SELECTED
    lane = lax.broadcasted_iota(jnp.int32, (NE, TB), 1)
    acc_scr[...] = jnp.zeros_like(acc_scr)
    part_scr[...] = jnp.zeros_like(part_scr)

    def fold(t):
        act = jnp.sum(part_scr[...], axis=-1, keepdims=True)
        acc_scr[...] = jnp.where(lane == t, act, acc_scr[...])

    def per_token(t, _):
        fold(t - 1)
        xt = x_ref[t]
        xh = xt[:HALF_ROWS]
        xl = xt[HALF_ROWS:]
        for e in range(NE):
            hi, lo = _unpack(tab_ref[idx_ref[t, e]])
            prod_scr[e * HALF_ROWS:(e + 1) * HALF_ROWS, :] = hi * xh + lo * xl
        part = prod_scr[pl.ds(0, NE, stride=HALF_ROWS), :]
        for s in range(1, HALF_ROWS):
            part = part + prod_scr[pl.ds(s, NE, stride=HALF_ROWS), :]
        part_scr[...] = part
        return 0

    lax.fori_loop(0, TB, per_token, 0)
    fold(TB - 1)
    a = acc_scr[...]
    gelu = 0.5 * a * (1.0 + lax.erf(a * np.float32(1.0 / np.sqrt(2.0))))
    c_ref[...] = (gate_ref[...] * gelu).T


def _peer_act(idx, h2_3d, gate_t, u_packed, start, count):
    T, NE = idx.shape
    TB = PEER_TOKEN_TILE
    nrow = D_MODEL // LANES
    first = start // TB
    return pl.pallas_call(
        _peer_act_kernel,
        grid=(count // TB,),
        in_specs=[
            pl.BlockSpec((TB, NE), lambda i: (first + i, 0), memory_space=pltpu.SMEM),
            pl.BlockSpec((TB, nrow, LANES), lambda i: (first + i, 0, 0)),
            pl.BlockSpec((NE, TB), lambda i: (0, first + i)),
            pl.BlockSpec(u_packed.shape, lambda i: (0, 0, 0), pipeline_mode=pl.Buffered(1)),
        ],
        out_specs=pl.BlockSpec((TB, NE), lambda i: (i, 0)),
        out_shape=jax.ShapeDtypeStruct((count, NE), jnp.float32),
        scratch_shapes=[pltpu.VMEM((NE * HALF_ROWS, LANES), jnp.float32), pltpu.VMEM((NE, LANES), jnp.float32),
                        pltpu.VMEM((NE, TB), jnp.float32)],
        compiler_params=_cparams(("arbitrary",)),
        name="peer_act",
    )(idx, h2_3d, gate_t, u_packed)


GATHER_STRIDE = PEER_SELECTED + SUBLANES


def _peer_out_kernel_inplace(idx_ref, c_ref, x1_ref, g_ref, tab_ref, buf_ref, o_ref, gat_a, gat_b):
    del buf_ref
    _peer_out_kernel(idx_ref, c_ref, x1_ref, g_ref, tab_ref, o_ref, gat_a, gat_b)


def _peer_out_kernel(idx_ref, c_ref, x1_ref, g_ref, tab_ref, o_ref, gat_a, gat_b):
    TB = PEER_TOKEN_TILE
    NE = PEER_SELECTED
    GS = GATHER_STRIDE
    sub = lax.broadcasted_iota(jnp.int32, (SUBLANES, NE), 0)

    def gather(buf, t):
        for e in range(NE):
            buf[pl.ds(e, HALF_ROWS, stride=GS), :] = tab_ref[idx_ref[t, e]]

    def tile(buf):
        his, los = [], []
        for s in range(HALF_ROWS):
            hi, lo = _unpack(buf[s * GS:s * GS + NE, :])
            his.append(hi.astype(jnp.bfloat16))
            los.append(lo.astype(jnp.bfloat16))
        return jnp.concatenate(his + los, axis=1)

    def per_group(gi, _):
        base = pl.multiple_of(gi * SUBLANES, SUBLANES)
        c8 = c_ref[pl.ds(base, SUBLANES), :]
        y8 = jnp.zeros((SUBLANES, D_MODEL), jnp.float32)
        for tt in range(SUBLANES):
            buf = gat_a if tt % 2 == 0 else gat_b
            gather(buf, base + tt)
            ch, cl = _split(jnp.where(sub == tt, c8, 0.0))
            out = jnp.dot(jnp.concatenate([ch, cl], axis=0), tile(buf), preferred_element_type=jnp.float32)
            y8 = y8 + (out[:SUBLANES] + out[SUBLANES:])
        z = x1_ref[pl.ds(base, SUBLANES), :] + y8
        ms = jnp.mean(z * z, axis=-1, keepdims=True)
        o_ref[pl.ds(base, SUBLANES), :] = z * lax.rsqrt(ms + RMS_EPS) * g_ref[...]
        return 0

    lax.fori_loop(0, TB // SUBLANES, per_group, 0)


def _peer_out(idx, c, x1, lnf_g, v_packed, n_tokens, out_buf, out_row, total_rows):
    T, NE = idx.shape
    TB = PEER_TOKEN_TILE
    gat = pltpu.VMEM((HALF_ROWS * GATHER_STRIDE, LANES), jnp.uint32)
    first = out_row // TB
    in_specs = [
        pl.BlockSpec((TB, NE), lambda i: (i, 0), memory_space=pltpu.SMEM),
        pl.BlockSpec((TB, NE), lambda i: (i, 0)),
        pl.BlockSpec((TB, D_MODEL), lambda i: (i, 0)),
        pl.BlockSpec((1, D_MODEL), lambda i: (0, 0)),
        pl.BlockSpec(v_packed.shape, lambda i: (0, 0, 0), pipeline_mode=pl.Buffered(1)),
    ]
    args = [idx, c, x1, lnf_g.reshape(1, D_MODEL), v_packed]
    body = _peer_out_kernel
    aliases = {}
    if out_buf is not None:
        in_specs.append(pl.BlockSpec(memory_space=pl.ANY))
        args.append(out_buf)
        aliases = {len(args) - 1: 0}
        body = _peer_out_kernel_inplace
    return pl.pallas_call(
        body,
        grid=(n_tokens // TB,),
        in_specs=in_specs,
        out_specs=pl.BlockSpec((TB, D_MODEL), lambda i: (first + i, 0)),
        out_shape=jax.ShapeDtypeStruct((total_rows, D_MODEL), jnp.float32),
        scratch_shapes=[gat, gat],
        input_output_aliases=aliases,
        compiler_params=_cparams(("arbitrary",)),
        name="peer_out",
    )(*args)


SC_LANES = 16
SC_WORKERS = 32
SC_GATHER_ROWS = 32
SC_TOKEN_ALIGN = 256


SC_GROUPS = ((8, 16), (8, 16), (8, 16), (8, 4))


def _sc_tokens(tokens, share):
    return (tokens * share // 16) // SC_TOKEN_ALIGN * SC_TOKEN_ALIGN


def _peer_out_sc(idx, c, tab):
    n, NE = idx.shape
    D = tab.shape[1] * tab.shape[2]
    L = SC_LANES
    R = SC_GATHER_ROWS
    nrow = D // LANES
    per_w = n // SC_WORKERS
    mesh = plsc.VectorSubcoreMesh(core_axis_name="c", subcore_axis_name="s")

    @functools.partial(
        pl.kernel, mesh=mesh, compiler_params=pltpu.CompilerParams(needs_layout_passes=False),
        out_type=jax.ShapeDtypeStruct((n, nrow, LANES), jnp.float32),
        scratch_types=[
            pltpu.VMEM((NE,), jnp.int32),
            pltpu.VMEM((NE,), jnp.float32),
            pltpu.VMEM((2, R, nrow, LANES), jnp.float32),
            pltpu.VMEM((nrow, LANES), jnp.float32),
            pltpu.SemaphoreType.DMA((2,)),
        ],
    )
    def sc_kernel(idx_hbm, c_hbm, tab_hbm, out_hbm, idx_v, c_v, rows_v, y_v, sems):
        wid = lax.axis_index("s") * 2 + lax.axis_index("c")

        def gather(ci, b):
            return pltpu.make_async_copy(tab_hbm.at[idx_v.at[pl.ds(ci * R, R)]], rows_v.at[b], sems.at[b])

        @pl.loop(0, per_w)
        def _(i):
            t = wid * per_w + i
            pltpu.sync_copy(idx_hbm.at[t], idx_v)
            pltpu.sync_copy(c_hbm.at[t], c_v)
            gather(0, 0).start()
            for ci in range(NE // R):
                b = ci % 2
                if ci + 1 < NE // R:
                    gather(ci + 1, 1 - b).start()
                gather(ci, b).wait()
                for g in range(nrow):
                    cols = [pl.ds(j * L, L) for j in range(LANES // L)]
                    if ci == 0:
                        accs = tuple(jnp.zeros((L,), jnp.float32) for _ in cols)
                    else:
                        accs = tuple(y_v[g, cs] for cs in cols)

                    def row_body(e, accs, b=b, ci=ci, cols=cols, g=g):
                        ce = plsc.load_gather(c_v, [lax.iota(jnp.int32, L) * 0 + (ci * R + e)])
                        return tuple(a + ce * rows_v[b, e, g, cs] for a, cs in zip(accs, cols))

                    accs = lax.fori_loop(0, R, row_body, accs)
                    for a, cs in zip(accs, cols):
                        y_v[g, cs] = a
            pltpu.sync_copy(y_v, out_hbm.at[t])

    return sc_kernel(idx, c, tab).reshape(n, D)


def _tail_norm_kernel(buf_ref, x1_ref, y_ref, g_ref, o_ref):
    del buf_ref
    z = x1_ref[...] + y_ref[...]
    ms = jnp.mean(z * z, axis=-1, keepdims=True)
    o_ref[...] = z * lax.rsqrt(ms + RMS_EPS) * g_ref[...]


def _tail_norm(out_buf, x1, y_tail, lnf_g, tm, start, out_row):
    T, D = x1.shape
    n = y_tail.shape[0]
    first_in = start // tm
    first_out = out_row // tm
    return pl.pallas_call(
        _tail_norm_kernel,
        grid=(n // tm,),
        in_specs=[
            pl.BlockSpec(memory_space=pl.ANY),
            pl.BlockSpec((tm, D), lambda i: (first_in + i, 0)),
            pl.BlockSpec((tm, D), lambda i: (i, 0)),
            pl.BlockSpec((1, D), lambda i: (0, 0)),
        ],
        out_specs=pl.BlockSpec((tm, D), lambda i: (first_out + i, 0)),
        out_shape=jax.ShapeDtypeStruct(out_buf.shape, jnp.float32),
        input_output_aliases={0: 0},
        compiler_params=_cparams(("parallel",)),
        name="peer_tail_norm",
    )(out_buf, x1, y_tail, lnf_g.reshape(1, D))


def _block(x2, first_row, B, S, out_buf, sc_share, ln1_g, w_in, b_attn, mu_shift, w0, w_up, a0, a_up, g_up,
           k_k, k_a, r_k, lnx_w, lnx_b, attn_sinks, attn_norm_g, w_out, ln2_g, peer_wq, peer_subkeys,
           u_packed, v_packed, v_rows, lnf_g):
    T = B * S
    p_rkv, p_lora, p_attn = _inproj(x2, ln1_g, w_in, b_attn, 512, first_row, T)
    y_rwkv = _rwkv(p_rkv.reshape(B, S, -1), p_lora.reshape(B, S, -1), mu_shift, w0, w_up, a0,
                   a_up, g_up, k_k, k_a, r_k.reshape(-1), lnx_w, lnx_b)
    y_attn = _attention(p_attn.reshape(B, S, -1), attn_sinks, attn_norm_g)
    x1, h2_3d, scores_t = _outproj(x2, y_rwkv.reshape(T, -1), y_attn.reshape(T, -1), w_out, ln2_g, peer_wq,
                                   peer_subkeys, 512, first_row)
    idx, gate_t = _topk(scores_t, tm=512)
    n_sc = _sc_tokens(T, sc_share)
    n_tc = T - n_sc
    tails = []
    if n_sc:
        c_sc = _peer_act(idx, h2_3d, gate_t, u_packed, n_tc, n_sc)
        c_sc, gate_t = lax.optimization_barrier((c_sc, gate_t))
        tails.append((x1, n_tc, _peer_out_sc(idx[n_tc:], c_sc, v_rows), first_row + n_tc))
    if n_tc:
        c = _peer_act(idx, h2_3d, gate_t, u_packed, 0, n_tc)
        out_buf = _peer_out(idx, c, x1, lnf_g, v_packed, n_tc, out_buf, first_row, x2.shape[0])
    return out_buf, tails


def kernel(x, ln1_g, w_in, b_attn, mu_shift, w0, w_up, a0, a_up, g_up, k_k, k_a, r_k, lnx_w, lnx_b, attn_sinks, attn_norm_g, w_out, ln2_g, peer_wq, peer_subkeys, peer_u, peer_v, lnf_g):
    B, S, D = x.shape
    groups = SC_GROUPS if B % 32 == 0 else ((32, 8),)
    x2 = x.reshape(B * S, D)
    u_packed = _pack_table(peer_u[0])
    v_packed = _pack_table(peer_v[0])
    v_rows = peer_v[0].reshape(peer_v.shape[1], D // LANES, LANES)
    out, tails, row = None, [], 0
    for frac, share in groups:
        rows = B * frac // 32
        out, tail = _block(x2, row * S, rows, S, out, share, ln1_g[0], w_in[0], b_attn[0], mu_shift[0],
                           w0[0], w_up[0], a0[0], a_up[0], g_up[0], k_k[0], k_a[0], r_k[0], lnx_w[0], lnx_b[0],
                           attn_sinks[0], attn_norm_g[0], w_out[0], ln2_g[0], peer_wq[0], peer_subkeys[0],
                           u_packed, v_packed, v_rows, lnf_g)
        tails += tail
        row += rows
    for x1, start, y, out_row in tails:
        out = _tail_norm(out, x1, y, lnf_g, SC_TOKEN_ALIGN, start, out_row)
    return out.reshape(B, S, D)
```

```python
import functools

import jax
import jax.numpy as jnp
import numpy as np
from jax import lax
from jax.experimental import pallas as pl
from jax.experimental.pallas import tpu as pltpu
from jax.experimental.pallas import tpu_sc as plsc

D_MODEL = 1024
HEAD_DIM = 64
RWKV_WIDTH = 512
RWKV_HEADS = 8
ATTN_WIDTH = 512
ATTN_Q_HEADS = 8
ATTN_KV_HEADS = 2
ATTN_GROUP = 4
ATTN_KV_WIDTH = 128
WINDOW = 128
D_DECAY_LORA = 32
D_AAA_LORA = 32
D_GATE_LORA = 96
LORA_COLS = D_DECAY_LORA + D_AAA_LORA + D_GATE_LORA
LORA_PAD = 256
RKV_COLS = 3 * RWKV_WIDTH
ATTN_COLS = ATTN_WIDTH + 2 * ATTN_KV_WIDTH
PEER_HEADS = 8
PEER_N_KEYS = 128
PEER_HALF = 128
PEER_TOPK = 16
RMS_EPS = 1e-6
LNX_EPS = 64e-5
NEG_INF = -1e30

LANES = 128
SUBLANES = 8
VMEM_LIMIT_BYTES = 56 * 1024 * 1024

RWKV_CHUNK = 64
HEAD_PAIR = 2 * HEAD_DIM

_HI = lax.Precision.HIGHEST
_NN = (((1,), (0,)), ((), ()))
_NT = (((1,), (1,)), ((), ()))
_TN = (((0,), (0,)), ((), ()))


def _split(a):
    hi = a.astype(jnp.bfloat16)
    lo = (a - hi.astype(jnp.float32)).astype(jnp.bfloat16)
    return hi, lo


def _dot32(a, b, dims=_NN):
    ah, al = _split(a)
    bh, bl = _split(b)
    d = lambda p, q: lax.dot_general(p, q, dims, preferred_element_type=jnp.float32)
    return d(ah, bh) + (d(ah, bl) + d(al, bh))


def _dot16(a, b, dims=_NN):
    return lax.dot_general(a.astype(jnp.bfloat16), b.astype(jnp.bfloat16), dims,
                           preferred_element_type=jnp.float32)


def _cparams(sem):
    return pltpu.CompilerParams(dimension_semantics=sem, vmem_limit_bytes=VMEM_LIMIT_BYTES)


def _inproj_kernel(x_ref, g_ref, w_ref, b_ref, rkv_ref, lora_ref, attn_ref):
    x = x_ref[...]
    ms = jnp.mean(x * x, axis=-1, keepdims=True)
    h = (x * lax.rsqrt(ms + RMS_EPS) * g_ref[...]).astype(jnp.bfloat16)
    p = jnp.dot(h, w_ref[...], preferred_element_type=jnp.float32)
    rkv_ref[...] = p[:, :RKV_COLS]
    lora_ref[...] = p[:, RKV_COLS:RKV_COLS + LORA_PAD]
    attn_ref[...] = p[:, RKV_COLS + LORA_PAD:] + b_ref[...]


def _inproj(x2, ln1_g, w_in, b_attn, tm, first_row, T):
    first = first_row // tm
    w_rkv = w_in[:, :RKV_COLS]
    w_lora = jnp.pad(w_in[:, RKV_COLS:RKV_COLS + LORA_COLS], ((0, 0), (0, LORA_PAD - LORA_COLS)))
    w_attn = w_in[:, RKV_COLS + LORA_COLS:]
    w_all = jnp.concatenate([w_rkv, w_lora, w_attn], axis=1).astype(jnp.bfloat16)
    ncols = w_all.shape[1]
    return pl.pallas_call(
        _inproj_kernel,
        grid=(T // tm,),
        in_specs=[
            pl.BlockSpec((tm, D_MODEL), lambda i: (first + i, 0)),
            pl.BlockSpec((1, D_MODEL), lambda i: (0, 0)),
            pl.BlockSpec((D_MODEL, ncols), lambda i: (0, 0)),
            pl.BlockSpec((1, ATTN_COLS), lambda i: (0, 0)),
        ],
        out_specs=[
            pl.BlockSpec((tm, RKV_COLS), lambda i: (i, 0)),
            pl.BlockSpec((tm, LORA_PAD), lambda i: (i, 0)),
            pl.BlockSpec((tm, ATTN_COLS), lambda i: (i, 0)),
        ],
        out_shape=[
            jax.ShapeDtypeStruct((T, RKV_COLS), jnp.float32),
            jax.ShapeDtypeStruct((T, LORA_PAD), jnp.float32),
            jax.ShapeDtypeStruct((T, ATTN_COLS), jnp.float32),
        ],
        compiler_params=_cparams(("parallel",)),
        name="inproj",
    )(x2, ln1_g.reshape(1, D_MODEL), w_all, b_attn.reshape(1, ATTN_COLS))


def _rwkv_kernel(r_ref, k_ref, v_ref, lora_ref, mur_ref, muk_ref, muv_ref, mul_ref,
                 w0_ref, wup_ref, a0_ref, aup_ref, gup_ref, kk_ref, ka_ref, rk_ref,
                 lnw_ref, lnb_ref, o_ref, s_scr, rkv_buf, lora_buf):
    C = RWKV_CHUNK
    P = HEAD_PAIR
    f32 = jnp.float32
    c = pl.program_id(1)

    @pl.when(c == 0)
    def _():
        s_scr[...] = jnp.zeros_like(s_scr)
        rkv_buf[...] = jnp.zeros_like(rkv_buf)
        lora_buf[...] = jnp.zeros_like(lora_buf)

    def shifted(buf, j, cur, mu):
        buf[j, SUBLANES:SUBLANES + C, :] = cur
        prev = buf[j, SUBLANES - 1:SUBLANES - 1 + C, :]
        buf[j, SUBLANES - 1:SUBLANES, :] = cur[C - 1:C, :]
        return cur + mu * (prev - cur)

    r = shifted(rkv_buf, 0, r_ref[0], mur_ref[...])
    k = shifted(rkv_buf, 1, k_ref[0], muk_ref[...])
    v = shifted(rkv_buf, 2, v_ref[0], muv_ref[...])
    xl = shifted(lora_buf, 0, lora_ref[0], mul_ref[...])

    w_raw = w0_ref[...] + _dot16(jnp.tanh(xl), wup_ref[...])
    z = -w_raw
    softplus = jnp.maximum(z, 0.0) + jnp.log(1.0 + jnp.exp(-jnp.abs(z)))
    lw = -jnp.exp(-softplus - 0.5)
    a = jax.nn.sigmoid(a0_ref[...] + _dot16(xl, aup_ref[...]))
    g = _dot32(jax.nn.sigmoid(xl), gup_ref[...])

    lane = lax.broadcasted_iota(jnp.int32, (1, P), 1)
    m0 = lane < HEAD_DIM
    rowp = lax.broadcasted_iota(jnp.int32, (P, P), 0)
    colp = lax.broadcasted_iota(jnp.int32, (P, P), 1)
    same_head = (rowp < HEAD_DIM) == (colp < HEAD_DIM)
    bd = jnp.where(same_head, 1.0, 0.0).astype(jnp.bfloat16)
    strict2 = same_head & ((rowp & (C - 1)) > (colp & (C - 1)))
    incl2 = same_head & ((rowp & (C - 1)) >= (colp & (C - 1)))
    ti = lax.broadcasted_iota(jnp.int32, (C, C), 0)
    tj = lax.broadcasted_iota(jnp.int32, (C, C), 1)
    tril = jnp.where(ti >= tj, 1.0, 0.0).astype(f32)
    head_sum = lambda t: jnp.concatenate(
        [_dot16(t[:, p * P:(p + 1) * P], bd) for p in range(t.shape[1] // P)], axis=1)

    kk = k * kk_ref[...]
    kk = kk / jnp.maximum(jnp.sqrt(head_sum(kk * kk)), 1e-12)
    k2 = k * (1.0 + (a - 1.0) * ka_ref[...])
    bonus = head_sum(r * k2 * rk_ref[...]) * v

    lc = _dot32(tril, lw)
    e_pos = jnp.exp(lc)
    e_neg = jnp.exp(-lc)
    rt_all = r * e_pos
    at_all = -kk * jnp.exp(lc - lw)
    bt_all = kk * a * e_neg
    kt_all = k2 * e_neg

    twice = lambda t: jnp.concatenate([t, t], axis=0)
    stack = lambda t: jnp.concatenate([jnp.where(m0, t, 0.0), jnp.where(m0, 0.0, t)], axis=0)
    pairs = range(r.shape[1] // P)
    cut = lambda t: [t[:, p * P:(p + 1) * P] for p in pairs]
    at, rt, bt, kt, vp = cut(at_all), cut(rt_all), cut(bt_all), cut(kt_all), cut(v)
    s0 = [s_scr[p] for p in pairs]
    bk = [jnp.concatenate([stack(bt[p]), stack(kt[p])], axis=0) for p in pairs]
    pa = [_dot16(stack(at[p]), bk[p], _NT) for p in pairs]
    m = [jnp.where(strict2, pa[p][:, :P], 0.0) for p in pairs]
    v2 = [twice(vp[p]) for p in pairs]
    x = [twice(_dot16(at[p], s0[p], _NT)) + _dot16(jnp.where(strict2, pa[p][:, P:], 0.0), v2[p])
         for p in pairs]
    for step in range(6):
        x = [x[p] + _dot16(m[p], x[p]) for p in pairs]
        if step < 5:
            m = [_dot16(m[p], m[p]) for p in pairs]
    pr = [_dot32(stack(rt[p]), bk[p], _NT) for p in pairs]
    nr = [jnp.concatenate([jnp.where(incl2, pr[p][:, :P], 0.0), jnp.where(incl2, pr[p][:, P:], 0.0)], axis=1)
          for p in pairs]
    y2 = [twice(_dot16(rt[p], s0[p], _NT)) + _dot32(nr[p], jnp.concatenate([x[p], v2[p]], axis=0))
          for p in pairs]
    u = [jnp.where(m0, x[p][:C], x[p][C:]) for p in pairs]
    ys = [jnp.where(m0, y2[p][:C], y2[p][C:]) for p in pairs]
    for p in pairs:
        ds = _dot32(jnp.concatenate([u[p], vp[p]], axis=0), jnp.concatenate([bt[p], kt[p]], axis=0), _TN)
        s_scr[p] = jnp.where(same_head, s0[p] + ds, 0.0) * e_pos[C - 1:C, p * P:(p + 1) * P]
    y = jnp.concatenate(ys, axis=1)

    mean = head_sum(y) * (1.0 / HEAD_DIM)
    yc = y - mean
    var = head_sum(yc * yc) * (1.0 / HEAD_DIM)
    yn = yc * lax.rsqrt(var + LNX_EPS) * lnw_ref[...] + lnb_ref[...]
    o_ref[0] = (yn + bonus) * g


def _rwkv(p_rkv, p_lora, mu_shift, w0, w_up, a0, a_up, g_up, k_k, k_a, r_k, lnx_w, lnx_b):
    B, S, _ = p_rkv.shape
    C = RWKV_CHUNK
    W = RWKV_WIDTH
    row = lambda t: t.reshape(1, -1)
    mu_r, mu_k, mu_v = (row(mu_shift[i * W:(i + 1) * W]) for i in range(3))
    mu_l = row(jnp.pad(mu_shift[3 * W:], (0, LORA_PAD - LORA_COLS)))
    wup_p = jnp.zeros((LORA_PAD, W), jnp.float32).at[:D_DECAY_LORA].set(w_up)
    aup_p = jnp.zeros((LORA_PAD, W), jnp.float32).at[D_DECAY_LORA:D_DECAY_LORA + D_AAA_LORA].set(a_up)
    gup_p = jnp.zeros((LORA_PAD, W), jnp.float32).at[D_DECAY_LORA + D_AAA_LORA:LORA_COLS].set(g_up)
    vec = lambda: pl.BlockSpec((1, W), lambda b, c: (0, 0))
    mat = lambda: pl.BlockSpec((LORA_PAD, W), lambda b, c: (0, 0))
    col = lambda j: pl.BlockSpec((1, C, W), lambda b, c, j=j: (b, c, j))
    return pl.pallas_call(
        _rwkv_kernel,
        grid=(B, S // C),
        in_specs=[
            col(0), col(1), col(2),
            pl.BlockSpec((1, C, LORA_PAD), lambda b, c: (b, c, 0)),
            vec(), vec(), vec(),
            pl.BlockSpec((1, LORA_PAD), lambda b, c: (0, 0)),
            vec(), mat(), vec(), mat(), mat(), vec(), vec(), vec(), vec(), vec(),
        ],
        out_specs=pl.BlockSpec((1, C, W), lambda b, c: (b, c, 0)),
        out_shape=jax.ShapeDtypeStruct((B, S, W), jnp.float32),
        scratch_shapes=[
            pltpu.VMEM((W // HEAD_PAIR, HEAD_PAIR, HEAD_PAIR), jnp.float32),
            pltpu.VMEM((3, SUBLANES + C, W), jnp.float32),
            pltpu.VMEM((1, SUBLANES + C, LORA_PAD), jnp.float32),
        ],
        compiler_params=_cparams(("parallel", "arbitrary")),
        name="rwkv7",
    )(p_rkv, p_rkv, p_rkv, p_lora, mu_r, mu_k, mu_v, mu_l,
      row(w0), wup_p, row(a0), aup_p, gup_p, row(k_k), row(k_a), row(r_k), row(lnx_w), row(lnx_b))


def _attn_kernel(sink_ref, q_ref, kp_ref, kc_ref, vp_ref, vc_ref, g_ref, o_ref):
    n = pl.program_id(1)
    Wn = WINDOW
    scale = 1.0 / np.sqrt(HEAD_DIM)
    q = q_ref[0]
    kext = jnp.concatenate([kp_ref[0], kc_ref[0]], axis=0)
    vext = jnp.concatenate([vp_ref[0], vc_ref[0]], axis=0)
    qi = lax.broadcasted_iota(jnp.int32, (Wn, 2 * Wn), 0)
    kj = lax.broadcasted_iota(jnp.int32, (Wn, 2 * Wn), 1)
    diff = qi + Wn - kj
    allowed = (diff >= 0) & (diff < Wn) & ((n > 0) | (kj >= Wn))
    outs = []
    for h in range(ATTN_Q_HEADS):
        kv = h // ATTN_GROUP
        qh = q[:, h * HEAD_DIM:(h + 1) * HEAD_DIM]
        kh = kext[:, kv * HEAD_DIM:(kv + 1) * HEAD_DIM]
        vh = vext[:, kv * HEAD_DIM:(kv + 1) * HEAD_DIM]
        s = _dot16(qh, kh, _NT) * scale
        s = jnp.where(allowed, s, NEG_INF)
        sink = sink_ref[h]
        m = jnp.maximum(jnp.max(s, axis=-1, keepdims=True), sink)
        p = jnp.exp(s - m)
        denom = jnp.sum(p, axis=-1, keepdims=True) + jnp.exp(sink - m)
        outs.append(_dot16(p, vh) / denom)
    o = jnp.concatenate(outs, axis=-1)
    ms = jnp.mean(o * o, axis=-1, keepdims=True)
    o_ref[0] = o * lax.rsqrt(ms + RMS_EPS) * g_ref[...]


def _attention(p_attn, sinks, norm_g):
    B, S, _ = p_attn.shape
    Wn = WINDOW
    nb = S // Wn
    kcol = ATTN_WIDTH // ATTN_KV_WIDTH
    prev = lambda b, n: (b, jnp.maximum(n - 1, 0))
    return pl.pallas_call(
        _attn_kernel,
        grid=(B, nb),
        in_specs=[
            pl.BlockSpec(memory_space=pltpu.SMEM),
            pl.BlockSpec((1, Wn, ATTN_WIDTH), lambda b, n: (b, n, 0)),
            pl.BlockSpec((1, Wn, ATTN_KV_WIDTH), lambda b, n: prev(b, n) + (kcol,)),
            pl.BlockSpec((1, Wn, ATTN_KV_WIDTH), lambda b, n: (b, n, kcol)),
            pl.BlockSpec((1, Wn, ATTN_KV_WIDTH), lambda b, n: prev(b, n) + (kcol + 1,)),
            pl.BlockSpec((1, Wn, ATTN_KV_WIDTH), lambda b, n: (b, n, kcol + 1)),
            pl.BlockSpec((1, ATTN_WIDTH), lambda b, n: (0, 0)),
        ],
        out_specs=pl.BlockSpec((1, Wn, ATTN_WIDTH), lambda b, n: (b, n, 0)),
        out_shape=jax.ShapeDtypeStruct((B, S, ATTN_WIDTH), jnp.float32),
        compiler_params=_cparams(("parallel", "parallel")),
        name="swa_attn",
    )(sinks, p_attn, p_attn, p_attn, p_attn, p_attn, norm_g.reshape(1, ATTN_WIDTH))


def _outproj_kernel(x_ref, yr_ref, ya_ref, wr_ref, wa_ref, g_ref, wq_ref, sk_ref, x1_ref, h2_ref, st_ref):
    x1 = (x_ref[...] + _dot16(yr_ref[...], wr_ref[...]) + _dot16(ya_ref[...], wa_ref[...]))
    x1_ref[...] = x1
    ms = jnp.mean(x1 * x1, axis=-1, keepdims=True)
    h2 = x1 * lax.rsqrt(ms + RMS_EPS) * g_ref[...]
    for j in range(D_MODEL // LANES):
        h2_ref[:, j, :] = h2[:, j * LANES:(j + 1) * LANES]
    q = _dot16(h2, wq_ref[...])
    for hc in range(2 * PEER_HEADS):
        st_ref[hc] = _dot16(sk_ref[hc], q[:, hc * PEER_HALF:(hc + 1) * PEER_HALF], _NT)


def _outproj(x2, y_rwkv, y_attn, w_out, ln2_g, peer_wq, peer_subkeys, tm, first_row):
    T = y_rwkv.shape[0]
    first = first_row // tm
    nq = peer_wq.shape[1]
    nhc = 2 * PEER_HEADS
    w_r = w_out[:RWKV_WIDTH].astype(jnp.bfloat16)
    w_a = w_out[RWKV_WIDTH:].astype(jnp.bfloat16)
    sk = peer_subkeys.reshape(nhc, PEER_N_KEYS, PEER_HALF).astype(jnp.bfloat16)
    full = lambda shape: pl.BlockSpec(shape, lambda i: (0,) * len(shape))
    return pl.pallas_call(
        _outproj_kernel,
        grid=(T // tm,),
        in_specs=[
            pl.BlockSpec((tm, D_MODEL), lambda i: (first + i, 0)),
            pl.BlockSpec((tm, RWKV_WIDTH), lambda i: (i, 0)),
            pl.BlockSpec((tm, ATTN_WIDTH), lambda i: (i, 0)),
            full((RWKV_WIDTH, D_MODEL)), full((ATTN_WIDTH, D_MODEL)), full((1, D_MODEL)),
            full((D_MODEL, nq)), full((nhc, PEER_N_KEYS, PEER_HALF)),
        ],
        out_specs=[
            pl.BlockSpec((tm, D_MODEL), lambda i: (i, 0)),
            pl.BlockSpec((tm, D_MODEL // LANES, LANES), lambda i: (i, 0, 0)),
            pl.BlockSpec((nhc, PEER_N_KEYS, tm), lambda i: (0, 0, i)),
        ],
        out_shape=[
            jax.ShapeDtypeStruct((T, D_MODEL), jnp.float32),
            jax.ShapeDtypeStruct((T, D_MODEL // LANES, LANES), jnp.float32),
            jax.ShapeDtypeStruct((nhc, PEER_N_KEYS, T), jnp.float32),
        ],
        compiler_params=_cparams(("parallel",)),
        name="outproj_peerq",
    )(x2, y_rwkv, y_attn, w_r, w_a, ln2_g.reshape(1, D_MODEL), peer_wq.astype(jnp.bfloat16), sk)


def _top16(s, order, payload=None):
    n, w = s.shape
    K = PEER_TOPK
    payload = order if payload is None else payload
    out_row = lax.broadcasted_iota(jnp.int32, (K, w), 0)
    cur = s
    vals = jnp.zeros((K, w), jnp.float32)
    picks = jnp.zeros((K, w), jnp.float32)
    for i in range(K):
        m = jnp.max(cur, axis=0, keepdims=True)
        first = jnp.min(jnp.where(cur == m, order, np.float32(1e9)), axis=0, keepdims=True)
        hit = order == first
        pick = first if payload is order else jnp.max(jnp.where(hit, payload, -1.0), axis=0, keepdims=True)
        vals = jnp.where(out_row == i, m, vals)
        picks = jnp.where(out_row == i, pick, picks)
        cur = jnp.where(hit, -jnp.inf, cur)
    return vals, picks


def _staircase():
    K = PEER_TOPK
    pairs = [(a, b) for a in range(K) for b in range(K) if (a + 1) * (b + 1) <= K]
    n = -(-len(pairs) // SUBLANES) * SUBLANES
    sel = np.zeros((2, n, K), np.float32)
    pos = np.full((n, LANES), -1.0, np.float32)
    for r, (a, b) in enumerate(pairs):
        sel[0, r, a] = 1.0
        sel[1, r, b] = 1.0
        pos[r] = a * K + b
    return sel, pos


def _split3(x):
    h1 = x.astype(jnp.bfloat16)
    r1 = x - h1.astype(jnp.float32)
    h2 = r1.astype(jnp.bfloat16)
    h3 = (r1 - h2.astype(jnp.float32)).astype(jnp.bfloat16)
    return h1, h2, h3


def _pick_rows(sel, x):
    d = lambda t: jnp.dot(sel, t, preferred_element_type=jnp.float32)
    h1, h2, h3 = _split3(x)
    return (d(h1) + d(h2)) + d(h3)


def _topk_kernel(st_ref, sel_ref, pos_ref, idx_ref, gate_ref):
    K = PEER_TOPK
    ncol = st_ref.shape[2] // LANES
    key_order = lax.broadcasted_iota(jnp.int32, (PEER_N_KEYS, LANES), 0).astype(jnp.float32)
    sel_a = sel_ref[0]
    sel_b = sel_ref[1]
    pos = pos_ref[...]
    live = pos >= 0.0

    def per_col(j, _):
        col = pl.ds(pl.multiple_of(j * LANES, LANES), LANES)
        picks = []
        for h in range(PEER_HEADS):
            s1, i1 = _top16(st_ref[2 * h, :, col], key_order)
            s2, i2 = _top16(st_ref[2 * h + 1, :, col], key_order)
            cand_s = jnp.where(live, _pick_rows(sel_a, s1) + _pick_rows(sel_b, s2), -jnp.inf)
            pick = lambda sel, t: jnp.dot(sel, t.astype(jnp.bfloat16), preferred_element_type=jnp.float32)
            cand_i = pick(sel_a, i1) * np.float32(PEER_N_KEYS) + pick(sel_b, i2)
            best_s, best_i = _top16(cand_s, pos, cand_i)
            e = jnp.exp(best_s - best_s[0:1])
            gate = e / jnp.sum(e, axis=0, keepdims=True)
            picks.append(best_i)
            gate_ref[h * K:(h + 1) * K, col] = gate
        idx_ref[col, :] = jnp.concatenate(picks, axis=0).T.astype(jnp.int32)
        return 0

    lax.fori_loop(0, ncol, per_col, 0)


def _topk(scores_t, tm):
    nhc, nk, T = scores_t.shape
    ne = PEER_HEADS * PEER_TOPK
    sel, pos = _staircase()
    return pl.pallas_call(
        _topk_kernel,
        grid=(T // tm,),
        in_specs=[pl.BlockSpec((nhc, nk, tm), lambda i: (0, 0, i)),
                  pl.BlockSpec(sel.shape, lambda i: (0, 0, 0)),
                  pl.BlockSpec(pos.shape, lambda i: (0, 0))],
        out_specs=[pl.BlockSpec((tm, ne), lambda i: (i, 0)), pl.BlockSpec((ne, tm), lambda i: (0, i))],
        out_shape=[jax.ShapeDtypeStruct((T, ne), jnp.int32), jax.ShapeDtypeStruct((ne, T), jnp.float32)],
        compiler_params=_cparams(("parallel",)),
        name="peer_topk",
    )(scores_t, jnp.asarray(sel, jnp.bfloat16), jnp.asarray(pos))


PEER_TOKEN_TILE = 128
PEER_SELECTED = PEER_HEADS * PEER_TOPK
HALF_ROWS = D_MODEL // 2 // LANES
_HI_MASK = np.uint32(0xFFFF0000)


def _pack_table(tab):
    n, d = tab.shape
    b = lax.bitcast_convert_type(tab.astype(jnp.bfloat16), jnp.uint16).astype(jnp.uint32)
    w = (b[:, :d // 2] << 16) | b[:, d // 2:]
    return w.reshape(n, d // 2 // LANES, LANES)


def _unpack(w):
    hi = lax.bitcast_convert_type(w & _HI_MASK, jnp.float32)
    lo = lax.bitcast_convert_type(w << 16, jnp.float32)
    return hi, lo


def _peer_act_kernel(idx_ref, x_ref, gate_ref, tab_ref, c_ref, prod_scr, part_scr, acc_scr):
    TB = PEER_TOKEN_TILE
    NE = PEER_SELECTED
    lane = lax.broadcasted_iota(jnp.int32, (NE, TB), 1)
    acc_scr[...] = jnp.zeros_like(acc_scr)
    part_scr[...] = jnp.zeros_like(part_scr)

    def fold(t):
        act = jnp.sum(part_scr[...], axis=-1, keepdims=True)
        acc_scr[...] = jnp.where(lane == t, act, acc_scr[...])

    def per_token(t, _):
        fold(t - 1)
        xt = x_ref[t]
        xh = xt[:HALF_ROWS]
        xl = xt[HALF_ROWS:]
        for e in range(NE):
            hi, lo = _unpack(tab_ref[idx_ref[t, e]])
            prod_scr[e * HALF_ROWS:(e + 1) * HALF_ROWS, :] = hi * xh + lo * xl
        part = prod_scr[pl.ds(0, NE, stride=HALF_ROWS), :]
        for s in range(1, HALF_ROWS):
            part = part + prod_scr[pl.ds(s, NE, stride=HALF_ROWS), :]
        part_scr[...] = part
        return 0

    lax.fori_loop(0, TB, per_token, 0)
    fold(TB - 1)
    a = acc_scr[...]
    gelu = 0.5 * a * (1.0 + lax.erf(a * np.float32(1.0 / np.sqrt(2.0))))
    c_ref[...] = (gate_ref[...] * gelu).T


def _peer_act(idx, h2_3d, gate_t, u_packed, start, count):
    T, NE = idx.shape
    TB = PEER_TOKEN_TILE
    nrow = D_MODEL // LANES
    first = start // TB
    return pl.pallas_call(
        _peer_act_kernel,
        grid=(count // TB,),
        in_specs=[
            pl.BlockSpec((TB, NE), lambda i: (first + i, 0), memory_space=pltpu.SMEM),
            pl.BlockSpec((TB, nrow, LANES), lambda i: (first + i, 0, 0)),
            pl.BlockSpec((NE, TB), lambda i: (0, first + i)),
            pl.BlockSpec(u_packed.shape, lambda i: (0, 0, 0), pipeline_mode=pl.Buffered(1)),
        ],
        out_specs=pl.BlockSpec((TB, NE), lambda i: (i, 0)),
        out_shape=jax.ShapeDtypeStruct((count, NE), jnp.float32),
        scratch_shapes=[pltpu.VMEM((NE * HALF_ROWS, LANES), jnp.float32), pltpu.VMEM((NE, LANES), jnp.float32),
                        pltpu.VMEM((NE, TB), jnp.float32)],
        compiler_params=_cparams(("arbitrary",)),
        name="peer_act",
    )(idx, h2_3d, gate_t, u_packed)


GATHER_STRIDE = PEER_SELECTED + SUBLANES


def _peer_out_kernel_inplace(idx_ref, c_ref, x1_ref, g_ref, tab_ref, buf_ref, o_ref, gat_a, gat_b):
    del buf_ref
    _peer_out_kernel(idx_ref, c_ref, x1_ref, g_ref, tab_ref, o_ref, gat_a, gat_b)


def _peer_out_kernel(idx_ref, c_ref, x1_ref, g_ref, tab_ref, o_ref, gat_a, gat_b):
    TB = PEER_TOKEN_TILE
    NE = PEER_SELECTED
    GS = GATHER_STRIDE
    sub = lax.broadcasted_iota(jnp.int32, (SUBLANES, NE), 0)

    def gather(buf, t):
        for e in range(NE):
            buf[pl.ds(e, HALF_ROWS, stride=GS), :] = tab_ref[idx_ref[t, e]]

    def tile(buf):
        his, los = [], []
        for s in range(HALF_ROWS):
            hi, lo = _unpack(buf[s * GS:s * GS + NE, :])
            his.append(hi.astype(jnp.bfloat16))
            los.append(lo.astype(jnp.bfloat16))
        return jnp.concatenate(his + los, axis=1)

    def per_group(gi, _):
        base = pl.multiple_of(gi * SUBLANES, SUBLANES)
        c8 = c_ref[pl.ds(base, SUBLANES), :]
        y8 = jnp.zeros((SUBLANES, D_MODEL), jnp.float32)
        for tt in range(SUBLANES):
            buf = gat_a if tt % 2 == 0 else gat_b
            gather(buf, base + tt)
            ch, cl = _split(jnp.where(sub == tt, c8, 0.0))
            out = jnp.dot(jnp.concatenate([ch, cl], axis=0), tile(buf), preferred_element_type=jnp.float32)
            y8 = y8 + (out[:SUBLANES] + out[SUBLANES:])
        z = x1_ref[pl.ds(base, SUBLANES), :] + y8
        ms = jnp.mean(z * z, axis=-1, keepdims=True)
        o_ref[pl.ds(base, SUBLANES), :] = z * lax.rsqrt(ms + RMS_EPS) * g_ref[...]
        return 0

    lax.fori_loop(0, TB // SUBLANES, per_group, 0)


def _peer_out(idx, c, x1, lnf_g, v_packed, n_tokens, out_buf, out_row, total_rows):
    T, NE = idx.shape
    TB = PEER_TOKEN_TILE
    gat = pltpu.VMEM((HALF_ROWS * GATHER_STRIDE, LANES), jnp.uint32)
    first = out_row // TB
    in_specs = [
        pl.BlockSpec((TB, NE), lambda i: (i, 0), memory_space=pltpu.SMEM),
        pl.BlockSpec((TB, NE), lambda i: (i, 0)),
        pl.BlockSpec((TB, D_MODEL), lambda i: (i, 0)),
        pl.BlockSpec((1, D_MODEL), lambda i: (0, 0)),
        pl.BlockSpec(v_packed.shape, lambda i: (0, 0, 0), pipeline_mode=pl.Buffered(1)),
    ]
    args = [idx, c, x1, lnf_g.reshape(1, D_MODEL), v_packed]
    body = _peer_out_kernel
    aliases = {}
    if out_buf is not None:
        in_specs.append(pl.BlockSpec(memory_space=pl.ANY))
        args.append(out_buf)
        aliases = {len(args) - 1: 0}
        body = _peer_out_kernel_inplace
    return pl.pallas_call(
        body,
        grid=(n_tokens // TB,),
        in_specs=in_specs,
        out_specs=pl.BlockSpec((TB, D_MODEL), lambda i: (first + i, 0)),
        out_shape=jax.ShapeDtypeStruct((total_rows, D_MODEL), jnp.float32),
        scratch_shapes=[gat, gat],
        input_output_aliases=aliases,
        compiler_params=_cparams(("arbitrary",)),
        name="peer_out",
    )(*args)


SC_LANES = 16
SC_WORKERS = 32
SC_GATHER_ROWS = 32
SC_TOKEN_BLOCK = 4
SC_TOKEN_ALIGN = 256


SC_GROUPS = ((8, 16), (8, 16), (8, 16), (8, 10))


def _sc_tokens(tokens, share):
    return (tokens * share // 16) // SC_TOKEN_ALIGN * SC_TOKEN_ALIGN


def _peer_out_sc(idx, c, tab):
    n, NE = idx.shape
    D = tab.shape[1] * tab.shape[2]
    L = SC_LANES
    R = SC_GATHER_ROWS
    nrow = D // LANES
    TOK = SC_TOKEN_BLOCK
    nchunk = NE // R
    per_w = n // SC_WORKERS
    mesh = plsc.VectorSubcoreMesh(core_axis_name="c", subcore_axis_name="s")

    @functools.partial(
        pl.kernel, mesh=mesh, compiler_params=pltpu.CompilerParams(needs_layout_passes=False),
        out_type=jax.ShapeDtypeStruct((n, nrow, LANES), jnp.float32),
        scratch_types=[
            pltpu.VMEM((TOK, NE), jnp.int32),
            pltpu.VMEM((TOK, NE), jnp.float32),
            pltpu.VMEM((2, R, nrow, LANES), jnp.float32),
            pltpu.VMEM((2, nrow, LANES), jnp.float32),
            pltpu.SemaphoreType.DMA((2,)),
            pltpu.SemaphoreType.DMA((2,)),
        ],
    )
    def sc_kernel(idx_hbm, c_hbm, tab_hbm, out_hbm, idx_v, c_v, rows_v, y_v, sems, osems):
        wid = lax.axis_index("s") * 2 + lax.axis_index("c")

        def gather(tt, ci, b):
            return pltpu.make_async_copy(tab_hbm.at[idx_v.at[tt, pl.ds(ci * R, R)]], rows_v.at[b], sems.at[b])

        def put(t, yb):
            return pltpu.make_async_copy(y_v.at[yb], out_hbm.at[t], osems.at[yb])

        @pl.loop(0, per_w // TOK)
        def _(blk):
            t0 = wid * per_w + blk * TOK
            pltpu.sync_copy(idx_hbm.at[pl.ds(t0, TOK)], idx_v)
            pltpu.sync_copy(c_hbm.at[pl.ds(t0, TOK)], c_v)
            gather(0, 0, 0).start()
            for tt in range(TOK):
                yb = tt % 2
                if tt >= 2:
                    put(t0 + tt - 2, yb).wait()
                for ci in range(nchunk):
                    b = (tt * nchunk + ci) % 2
                    if ci + 1 < nchunk:
                        gather(tt, ci + 1, 1 - b).start()
                    elif tt + 1 < TOK:
                        gather(tt + 1, 0, 1 - b).start()
                    gather(tt, ci, b).wait()
                    for g in range(nrow):
                        cols = [pl.ds(j * L, L) for j in range(LANES // L)]
                        if ci == 0:
                            accs = tuple(jnp.zeros((L,), jnp.float32) for _ in cols)
                        else:
                            accs = tuple(y_v[yb, g, cs] for cs in cols)

                        def row_body(e, accs, b=b, ci=ci, cols=cols, g=g, tt=tt):
                            lane0 = lax.iota(jnp.int32, L) * 0
                            ce = plsc.load_gather(c_v, [lane0 + tt, lane0 + (ci * R + e)])
                            return tuple(a + ce * rows_v[b, e, g, cs] for a, cs in zip(accs, cols))

                        accs = lax.fori_loop(0, R, row_body, accs)
                        for a, cs in zip(accs, cols):
                            y_v[yb, g, cs] = a
                put(t0 + tt, yb).start()
            put(t0 + TOK - 2, 0).wait()
            put(t0 + TOK - 1, 1).wait()

    return sc_kernel(idx, c, tab).reshape(n, D)


def _tail_norm_kernel(buf_ref, x1_ref, y_ref, g_ref, o_ref):
    del buf_ref
    z = x1_ref[...] + y_ref[...]
    ms = jnp.mean(z * z, axis=-1, keepdims=True)
    o_ref[...] = z * lax.rsqrt(ms + RMS_EPS) * g_ref[...]


def _tail_norm(out_buf, x1, y_tail, lnf_g, tm, start, out_row):
    T, D = x1.shape
    n = y_tail.shape[0]
    first_in = start // tm
    first_out = out_row // tm
    return pl.pallas_call(
        _tail_norm_kernel,
        grid=(n // tm,),
        in_specs=[
            pl.BlockSpec(memory_space=pl.ANY),
            pl.BlockSpec((tm, D), lambda i: (first_in + i, 0)),
            pl.BlockSpec((tm, D), lambda i: (i, 0)),
            pl.BlockSpec((1, D), lambda i: (0, 0)),
        ],
        out_specs=pl.BlockSpec((tm, D), lambda i: (first_out + i, 0)),
        out_shape=jax.ShapeDtypeStruct(out_buf.shape, jnp.float32),
        input_output_aliases={0: 0},
        compiler_params=_cparams(("parallel",)),
        name="peer_tail_norm",
    )(out_buf, x1, y_tail, lnf_g.reshape(1, D))


def _block(x2, first_row, B, S, out_buf, sc_share, ln1_g, w_in, b_attn, mu_shift, w0, w_up, a0, a_up, g_up,
           k_k, k_a, r_k, lnx_w, lnx_b, attn_sinks, attn_norm_g, w_out, ln2_g, peer_wq, peer_subkeys,
           u_packed, v_packed, v_rows, lnf_g):
    T = B * S
    p_rkv, p_lora, p_attn = _inproj(x2, ln1_g, w_in, b_attn, 512, first_row, T)
    y_rwkv = _rwkv(p_rkv.reshape(B, S, -1), p_lora.reshape(B, S, -1), mu_shift, w0, w_up, a0,
                   a_up, g_up, k_k, k_a, r_k.reshape(-1), lnx_w, lnx_b)
    y_attn = _attention(p_attn.reshape(B, S, -1), attn_sinks, attn_norm_g)
    x1, h2_3d, scores_t = _outproj(x2, y_rwkv.reshape(T, -1), y_attn.reshape(T, -1), w_out, ln2_g, peer_wq,
                                   peer_subkeys, 512, first_row)
    idx, gate_t = _topk(scores_t, tm=512)
    n_sc = _sc_tokens(T, sc_share)
    n_tc = T - n_sc
    tails = []
    if n_sc:
        c_sc = _peer_act(idx, h2_3d, gate_t, u_packed, n_tc, n_sc)
        c_sc, gate_t = lax.optimization_barrier((c_sc, gate_t))
        tails.append((x1, n_tc, _peer_out_sc(idx[n_tc:], c_sc, v_rows), first_row + n_tc))
    if n_tc:
        c = _peer_act(idx, h2_3d, gate_t, u_packed, 0, n_tc)
        out_buf = _peer_out(idx, c, x1, lnf_g, v_packed, n_tc, out_buf, first_row, x2.shape[0])
    return out_buf, tails


def kernel(x, ln1_g, w_in, b_attn, mu_shift, w0, w_up, a0, a_up, g_up, k_k, k_a, r_k, lnx_w, lnx_b, attn_sinks, attn_norm_g, w_out, ln2_g, peer_wq, peer_subkeys, peer_u, peer_v, lnf_g):
    B, S, D = x.shape
    groups = SC_GROUPS if B % 32 == 0 else ((32, 8),)
    x2 = x.reshape(B * S, D)
    u_packed = _pack_table(peer_u[0])
    v_packed = _pack_table(peer_v[0])
    v_rows = peer_v[0].reshape(peer_v.shape[1], D // LANES, LANES)
    out, tails, row = None, [], 0
    for frac, share in groups:
        rows = B * frac // 32
        out, tail = _block(x2, row * S, rows, S, out, share, ln1_g[0], w_in[0], b_attn[0], mu_shift[0],
                           w0[0], w_up[0], a0[0], a_up[0], g_up[0], k_k[0], k_a[0], r_k[0], lnx_w[0], lnx_b[0],
                           attn_sinks[0], attn_norm_g[0], w_out[0], ln2_g[0], peer_wq[0], peer_subkeys[0],
                           u_packed, v_packed, v_rows, lnf_g)
        tails += tail
        row += rows
    for x1, start, y, out_row in tails:
        tm = 4 * SC_TOKEN_ALIGN
        while y.shape[0] % tm or start % tm or out_row % tm:
            tm //= 2
        out = _tail_norm(out, x1, y, lnf_g, tm, start, out_row)
    return out.reshape(B, S, D)
```

```python
import functools

import jax
import jax.numpy as jnp
import numpy as np
from jax import lax
from jax.experimental import pallas as pl
from jax.experimental.pallas import tpu as pltpu
from jax.experimental.pallas import tpu_sc as plsc

D_MODEL = 1024
HEAD_DIM = 64
RWKV_WIDTH = 512
RWKV_HEADS = 8
ATTN_WIDTH = 512
ATTN_Q_HEADS = 8
ATTN_KV_HEADS = 2
ATTN_GROUP = 4
ATTN_KV_WIDTH = 128
WINDOW = 128
D_DECAY_LORA = 32
D_AAA_LORA = 32
D_GATE_LORA = 96
LORA_COLS = D_DECAY_LORA + D_AAA_LORA + D_GATE_LORA
LORA_PAD = 256
RKV_COLS = 3 * RWKV_WIDTH
ATTN_COLS = ATTN_WIDTH + 2 * ATTN_KV_WIDTH
PEER_HEADS = 8
PEER_N_KEYS = 128
PEER_HALF = 128
PEER_TOPK = 16
RMS_EPS = 1e-6
LNX_EPS = 64e-5
NEG_INF = -1e30

LANES = 128
SUBLANES = 8
VMEM_LIMIT_BYTES = 56 * 1024 * 1024

RWKV_CHUNK = 64
HEAD_PAIR = 2 * HEAD_DIM

_NN = (((1,), (0,)), ((), ()))
_NT = (((1,), (1,)), ((), ()))
_TN = (((0,), (0,)), ((), ()))


def _split(a):
    hi = a.astype(jnp.bfloat16)
    lo = (a - hi.astype(jnp.float32)).astype(jnp.bfloat16)
    return hi, lo


def _dot32(a, b, dims=_NN):
    ah, al = _split(a)
    bh, bl = _split(b)
    d = lambda p, q: lax.dot_general(p, q, dims, preferred_element_type=jnp.float32)
    return d(ah, bh) + (d(ah, bl) + d(al, bh))


def _dot16(a, b, dims=_NN):
    return lax.dot_general(a.astype(jnp.bfloat16), b.astype(jnp.bfloat16), dims,
                           preferred_element_type=jnp.float32)


def _cparams(sem):
    return pltpu.CompilerParams(dimension_semantics=sem, vmem_limit_bytes=VMEM_LIMIT_BYTES)


def _inproj_kernel(x_ref, g_ref, w_ref, b_ref, rkv_ref, lora_ref, attn_ref):
    x = x_ref[...]
    ms = jnp.mean(x * x, axis=-1, keepdims=True)
    h = (x * lax.rsqrt(ms + RMS_EPS) * g_ref[...]).astype(jnp.bfloat16)
    p = jnp.dot(h, w_ref[...], preferred_element_type=jnp.float32)
    rkv_ref[...] = p[:, :RKV_COLS]
    lora_ref[...] = p[:, RKV_COLS:RKV_COLS + LORA_PAD]
    attn_ref[...] = p[:, RKV_COLS + LORA_PAD:] + b_ref[...]


def _inproj(x2, ln1_g, w_in, b_attn, tm, first_row, T):
    first = first_row // tm
    w_rkv = w_in[:, :RKV_COLS]
    w_lora = jnp.pad(w_in[:, RKV_COLS:RKV_COLS + LORA_COLS], ((0, 0), (0, LORA_PAD - LORA_COLS)))
    w_attn = w_in[:, RKV_COLS + LORA_COLS:]
    w_all = jnp.concatenate([w_rkv, w_lora, w_attn], axis=1).astype(jnp.bfloat16)
    ncols = w_all.shape[1]
    return pl.pallas_call(
        _inproj_kernel,
        grid=(T // tm,),
        in_specs=[
            pl.BlockSpec((tm, D_MODEL), lambda i: (first + i, 0)),
            pl.BlockSpec((1, D_MODEL), lambda i: (0, 0)),
            pl.BlockSpec((D_MODEL, ncols), lambda i: (0, 0)),
            pl.BlockSpec((1, ATTN_COLS), lambda i: (0, 0)),
        ],
        out_specs=[
            pl.BlockSpec((tm, RKV_COLS), lambda i: (i, 0)),
            pl.BlockSpec((tm, LORA_PAD), lambda i: (i, 0)),
            pl.BlockSpec((tm, ATTN_COLS), lambda i: (i, 0)),
        ],
        out_shape=[
            jax.ShapeDtypeStruct((T, RKV_COLS), jnp.float32),
            jax.ShapeDtypeStruct((T, LORA_PAD), jnp.float32),
            jax.ShapeDtypeStruct((T, ATTN_COLS), jnp.float32),
        ],
        compiler_params=_cparams(("parallel",)),
        name="inproj",
    )(x2, ln1_g.reshape(1, D_MODEL), w_all, b_attn.reshape(1, ATTN_COLS))


def _rwkv_kernel(r_ref, k_ref, v_ref, lora_ref, mur_ref, muk_ref, muv_ref, mul_ref,
                 w0_ref, wup_ref, a0_ref, aup_ref, gup_ref, kk_ref, ka_ref, rk_ref,
                 lnw_ref, lnb_ref, o_ref, s_scr, rkv_buf, lora_buf):
    C = RWKV_CHUNK
    P = HEAD_PAIR
    f32 = jnp.float32
    c = pl.program_id(1)

    @pl.when(c == 0)
    def _():
        s_scr[...] = jnp.zeros_like(s_scr)
        rkv_buf[...] = jnp.zeros_like(rkv_buf)
        lora_buf[...] = jnp.zeros_like(lora_buf)

    def shifted(buf, j, cur, mu):
        buf[j, SUBLANES:SUBLANES + C, :] = cur
        prev = buf[j, SUBLANES - 1:SUBLANES - 1 + C, :]
        buf[j, SUBLANES - 1:SUBLANES, :] = cur[C - 1:C, :]
        return cur + mu * (prev - cur)

    r = shifted(rkv_buf, 0, r_ref[0], mur_ref[...])
    k = shifted(rkv_buf, 1, k_ref[0], muk_ref[...])
    v = shifted(rkv_buf, 2, v_ref[0], muv_ref[...])
    xl = shifted(lora_buf, 0, lora_ref[0], mul_ref[...])

    w_raw = w0_ref[...] + _dot16(jnp.tanh(xl), wup_ref[...])
    z = -w_raw
    softplus = jnp.maximum(z, 0.0) + jnp.log(1.0 + jnp.exp(-jnp.abs(z)))
    lw = -jnp.exp(-softplus - 0.5)
    a = jax.nn.sigmoid(a0_ref[...] + _dot16(xl, aup_ref[...]))
    g = _dot32(jax.nn.sigmoid(xl), gup_ref[...])

    lane = lax.broadcasted_iota(jnp.int32, (1, P), 1)
    m0 = lane < HEAD_DIM
    rowp = lax.broadcasted_iota(jnp.int32, (P, P), 0)
    colp = lax.broadcasted_iota(jnp.int32, (P, P), 1)
    same_head = (rowp < HEAD_DIM) == (colp < HEAD_DIM)
    bd = jnp.where(same_head, 1.0, 0.0).astype(jnp.bfloat16)
    strict2 = same_head & ((rowp & (C - 1)) > (colp & (C - 1)))
    incl2 = same_head & ((rowp & (C - 1)) >= (colp & (C - 1)))
    ti = lax.broadcasted_iota(jnp.int32, (C, C), 0)
    tj = lax.broadcasted_iota(jnp.int32, (C, C), 1)
    tril = jnp.where(ti >= tj, 1.0, 0.0).astype(f32)
    head_sum = lambda t: jnp.concatenate(
        [_dot16(t[:, p * P:(p + 1) * P], bd) for p in range(t.shape[1] // P)], axis=1)

    kk = k * kk_ref[...]
    kk = kk / jnp.maximum(jnp.sqrt(head_sum(kk * kk)), 1e-12)
    k2 = k * (1.0 + (a - 1.0) * ka_ref[...])
    bonus = head_sum(r * k2 * rk_ref[...]) * v

    lc = _dot32(tril, lw)
    e_pos = jnp.exp(lc)
    e_neg = jnp.exp(-lc)
    rt_all = r * e_pos
    at_all = -kk * jnp.exp(lc - lw)
    bt_all = kk * a * e_neg
    kt_all = k2 * e_neg

    twice = lambda t: jnp.concatenate([t, t], axis=0)
    stack = lambda t: jnp.concatenate([jnp.where(m0, t, 0.0), jnp.where(m0, 0.0, t)], axis=0)
    pairs = range(r.shape[1] // P)
    cut = lambda t: [t[:, p * P:(p + 1) * P] for p in pairs]
    at, rt, bt, kt, vp = cut(at_all), cut(rt_all), cut(bt_all), cut(kt_all), cut(v)
    s0 = [s_scr[p] for p in pairs]
    bk = [jnp.concatenate([stack(bt[p]), stack(kt[p])], axis=0) for p in pairs]
    pa = [_dot16(stack(at[p]), bk[p], _NT) for p in pairs]
    m = [jnp.where(strict2, pa[p][:, :P], 0.0) for p in pairs]
    v2 = [twice(vp[p]) for p in pairs]
    x = [twice(_dot16(at[p], s0[p], _NT)) + _dot16(jnp.where(strict2, pa[p][:, P:], 0.0), v2[p])
         for p in pairs]
    for step in range(6):
        x = [x[p] + _dot16(m[p], x[p]) for p in pairs]
        if step < 5:
            m = [_dot16(m[p], m[p]) for p in pairs]
    pr = [_dot32(stack(rt[p]), bk[p], _NT) for p in pairs]
    nr = [jnp.concatenate([jnp.where(incl2, pr[p][:, :P], 0.0), jnp.where(incl2, pr[p][:, P:], 0.0)], axis=1)
          for p in pairs]
    y2 = [twice(_dot16(rt[p], s0[p], _NT)) + _dot32(nr[p], jnp.concatenate([x[p], v2[p]], axis=0))
          for p in pairs]
    u = [jnp.where(m0, x[p][:C], x[p][C:]) for p in pairs]
    ys = [jnp.where(m0, y2[p][:C], y2[p][C:]) for p in pairs]
    for p in pairs:
        ds = _dot32(jnp.concatenate([u[p], vp[p]], axis=0), jnp.concatenate([bt[p], kt[p]], axis=0), _TN)
        s_scr[p] = jnp.where(same_head, s0[p] + ds, 0.0) * e_pos[C - 1:C, p * P:(p + 1) * P]
    y = jnp.concatenate(ys, axis=1)

    mean = head_sum(y) * (1.0 / HEAD_DIM)
    yc = y - mean
    var = head_sum(yc * yc) * (1.0 / HEAD_DIM)
    yn = yc * lax.rsqrt(var + LNX_EPS) * lnw_ref[...] + lnb_ref[...]
    o_ref[0] = (yn + bonus) * g


def _rwkv(p_rkv, p_lora, mu_shift, w0, w_up, a0, a_up, g_up, k_k, k_a, r_k, lnx_w, lnx_b):
    B, S, _ = p_rkv.shape
    C = RWKV_CHUNK
    W = RWKV_WIDTH
    row = lambda t: t.reshape(1, -1)
    mu_r, mu_k, mu_v = (row(mu_shift[i * W:(i + 1) * W]) for i in range(3))
    mu_l = row(jnp.pad(mu_shift[3 * W:], (0, LORA_PAD - LORA_COLS)))
    wup_p = jnp.zeros((LORA_PAD, W), jnp.float32).at[:D_DECAY_LORA].set(w_up)
    aup_p = jnp.zeros((LORA_PAD, W), jnp.float32).at[D_DECAY_LORA:D_DECAY_LORA + D_AAA_LORA].set(a_up)
    gup_p = jnp.zeros((LORA_PAD, W), jnp.float32).at[D_DECAY_LORA + D_AAA_LORA:LORA_COLS].set(g_up)
    vec = lambda: pl.BlockSpec((1, W), lambda b, c: (0, 0))
    mat = lambda: pl.BlockSpec((LORA_PAD, W), lambda b, c: (0, 0))
    col = lambda j: pl.BlockSpec((1, C, W), lambda b, c, j=j: (b, c, j))
    return pl.pallas_call(
        _rwkv_kernel,
        grid=(B, S // C),
        in_specs=[
            col(0), col(1), col(2),
            pl.BlockSpec((1, C, LORA_PAD), lambda b, c: (b, c, 0)),
            vec(), vec(), vec(),
            pl.BlockSpec((1, LORA_PAD), lambda b, c: (0, 0)),
            vec(), mat(), vec(), mat(), mat(), vec(), vec(), vec(), vec(), vec(),
        ],
        out_specs=pl.BlockSpec((1, C, W), lambda b, c: (b, c, 0)),
        out_shape=jax.ShapeDtypeStruct((B, S, W), jnp.float32),
        scratch_shapes=[
            pltpu.VMEM((W // HEAD_PAIR, HEAD_PAIR, HEAD_PAIR), jnp.float32),
            pltpu.VMEM((3, SUBLANES + C, W), jnp.float32),
            pltpu.VMEM((1, SUBLANES + C, LORA_PAD), jnp.float32),
        ],
        compiler_params=_cparams(("parallel", "arbitrary")),
        name="rwkv7",
    )(p_rkv, p_rkv, p_rkv, p_lora, mu_r, mu_k, mu_v, mu_l,
      row(w0), wup_p, row(a0), aup_p, gup_p, row(k_k), row(k_a), row(r_k), row(lnx_w), row(lnx_b))


def _attn_kernel(sink_ref, q_ref, kp_ref, kc_ref, vp_ref, vc_ref, g_ref, o_ref):
    n = pl.program_id(1)
    Wn = WINDOW
    scale = 1.0 / np.sqrt(HEAD_DIM)
    q = q_ref[0]
    kext = jnp.concatenate([kp_ref[0], kc_ref[0]], axis=0)
    vext = jnp.concatenate([vp_ref[0], vc_ref[0]], axis=0)
    qi = lax.broadcasted_iota(jnp.int32, (Wn, 2 * Wn), 0)
    kj = lax.broadcasted_iota(jnp.int32, (Wn, 2 * Wn), 1)
    diff = qi + Wn - kj
    allowed = (diff >= 0) & (diff < Wn) & ((n > 0) | (kj >= Wn))
    outs = []
    for h in range(ATTN_Q_HEADS):
        kv = h // ATTN_GROUP
        qh = q[:, h * HEAD_DIM:(h + 1) * HEAD_DIM]
        kh = kext[:, kv * HEAD_DIM:(kv + 1) * HEAD_DIM]
        vh = vext[:, kv * HEAD_DIM:(kv + 1) * HEAD_DIM]
        s = _dot16(qh, kh, _NT) * scale
        s = jnp.where(allowed, s, NEG_INF)
        sink = sink_ref[h]
        m = jnp.maximum(jnp.max(s, axis=-1, keepdims=True), sink)
        p = jnp.exp(s - m)
        denom = jnp.sum(p, axis=-1, keepdims=True) + jnp.exp(sink - m)
        outs.append(_dot16(p, vh) / denom)
    o = jnp.concatenate(outs, axis=-1)
    ms = jnp.mean(o * o, axis=-1, keepdims=True)
    o_ref[0] = o * lax.rsqrt(ms + RMS_EPS) * g_ref[...]


def _attention(p_attn, sinks, norm_g):
    B, S, _ = p_attn.shape
    Wn = WINDOW
    nb = S // Wn
    kcol = ATTN_WIDTH // ATTN_KV_WIDTH
    prev = lambda b, n: (b, jnp.maximum(n - 1, 0))
    return pl.pallas_call(
        _attn_kernel,
        grid=(B, nb),
        in_specs=[
            pl.BlockSpec(memory_space=pltpu.SMEM),
            pl.BlockSpec((1, Wn, ATTN_WIDTH), lambda b, n: (b, n, 0)),
            pl.BlockSpec((1, Wn, ATTN_KV_WIDTH), lambda b, n: prev(b, n) + (kcol,)),
            pl.BlockSpec((1, Wn, ATTN_KV_WIDTH), lambda b, n: (b, n, kcol)),
            pl.BlockSpec((1, Wn, ATTN_KV_WIDTH), lambda b, n: prev(b, n) + (kcol + 1,)),
            pl.BlockSpec((1, Wn, ATTN_KV_WIDTH), lambda b, n: (b, n, kcol + 1)),
            pl.BlockSpec((1, ATTN_WIDTH), lambda b, n: (0, 0)),
        ],
        out_specs=pl.BlockSpec((1, Wn, ATTN_WIDTH), lambda b, n: (b, n, 0)),
        out_shape=jax.ShapeDtypeStruct((B, S, ATTN_WIDTH), jnp.float32),
        compiler_params=_cparams(("parallel", "parallel")),
        name="swa_attn",
    )(sinks, p_attn, p_attn, p_attn, p_attn, p_attn, norm_g.reshape(1, ATTN_WIDTH))


def _outproj_kernel(x_ref, yr_ref, ya_ref, wr_ref, wa_ref, g_ref, wq_ref, sk_ref, sel_ref, pos_ref,
                    x1_ref, h2_ref, idx_ref, gate_ref, st_ref):
    x1 = (x_ref[...] + _dot16(yr_ref[...], wr_ref[...]) + _dot16(ya_ref[...], wa_ref[...]))
    x1_ref[...] = x1
    ms = jnp.mean(x1 * x1, axis=-1, keepdims=True)
    h2 = x1 * lax.rsqrt(ms + RMS_EPS) * g_ref[...]
    for j in range(D_MODEL // LANES):
        h2_ref[:, j, :] = h2[:, j * LANES:(j + 1) * LANES]
    q = _dot16(h2, wq_ref[...])
    for hc in range(2 * PEER_HEADS):
        st_ref[hc] = _dot16(sk_ref[hc], q[:, hc * PEER_HALF:(hc + 1) * PEER_HALF], _NT)
    _topk_kernel(st_ref, sel_ref, pos_ref, idx_ref, gate_ref)


def _outproj(x2, y_rwkv, y_attn, w_out, ln2_g, peer_wq, peer_subkeys, tm, first_row):
    T = y_rwkv.shape[0]
    first = first_row // tm
    nq = peer_wq.shape[1]
    nhc = 2 * PEER_HEADS
    w_r = w_out[:RWKV_WIDTH].astype(jnp.bfloat16)
    w_a = w_out[RWKV_WIDTH:].astype(jnp.bfloat16)
    sk = peer_subkeys.reshape(nhc, PEER_N_KEYS, PEER_HALF).astype(jnp.bfloat16)
    full = lambda shape: pl.BlockSpec(shape, lambda i: (0,) * len(shape))
    ne = PEER_HEADS * PEER_TOPK
    sel, pos = _staircase()
    return pl.pallas_call(
        _outproj_kernel,
        grid=(T // tm,),
        in_specs=[
            pl.BlockSpec((tm, D_MODEL), lambda i: (first + i, 0)),
            pl.BlockSpec((tm, RWKV_WIDTH), lambda i: (i, 0)),
            pl.BlockSpec((tm, ATTN_WIDTH), lambda i: (i, 0)),
            full((RWKV_WIDTH, D_MODEL)), full((ATTN_WIDTH, D_MODEL)), full((1, D_MODEL)),
            full((D_MODEL, nq)), full((nhc, PEER_N_KEYS, PEER_HALF)), full(sel.shape), full(pos.shape),
        ],
        out_specs=[
            pl.BlockSpec((tm, D_MODEL), lambda i: (i, 0)),
            pl.BlockSpec((tm, D_MODEL // LANES, LANES), lambda i: (i, 0, 0)),
            pl.BlockSpec((tm, ne), lambda i: (i, 0)),
            pl.BlockSpec((ne, tm), lambda i: (0, i)),
        ],
        out_shape=[
            jax.ShapeDtypeStruct((T, D_MODEL), jnp.float32),
            jax.ShapeDtypeStruct((T, D_MODEL // LANES, LANES), jnp.float32),
            jax.ShapeDtypeStruct((T, ne), jnp.int32),
            jax.ShapeDtypeStruct((ne, T), jnp.float32),
        ],
        scratch_shapes=[pltpu.VMEM((nhc, PEER_N_KEYS, tm), jnp.float32)],
        compiler_params=_cparams(("parallel",)),
        name="outproj_peerq",
    )(x2, y_rwkv, y_attn, w_r, w_a, ln2_g.reshape(1, D_MODEL), peer_wq.astype(jnp.bfloat16), sk,
      jnp.asarray(sel, jnp.bfloat16), jnp.asarray(pos))


def _top16(s, order, payload=None):
    n, w = s.shape
    K = PEER_TOPK
    payload = order if payload is None else payload
    out_row = lax.broadcasted_iota(jnp.int32, (K, w), 0)
    cur = s
    vals = jnp.zeros((K, w), jnp.float32)
    picks = jnp.zeros((K, w), jnp.float32)
    for i in range(K):
        m = jnp.max(cur, axis=0, keepdims=True)
        first = jnp.min(jnp.where(cur == m, order, np.float32(1e9)), axis=0, keepdims=True)
        hit = order == first
        pick = first if payload is order else jnp.max(jnp.where(hit, payload, -1.0), axis=0, keepdims=True)
        vals = jnp.where(out_row == i, m, vals)
        picks = jnp.where(out_row == i, pick, picks)
        cur = jnp.where(hit, -jnp.inf, cur)
    return vals, picks


def _staircase():
    K = PEER_TOPK
    pairs = [(a, b) for a in range(K) for b in range(K) if (a + 1) * (b + 1) <= K]
    n = -(-len(pairs) // SUBLANES) * SUBLANES
    sel = np.zeros((2, n, K), np.float32)
    pos = np.full((n, LANES), -1.0, np.float32)
    for r, (a, b) in enumerate(pairs):
        sel[0, r, a] = 1.0
        sel[1, r, b] = 1.0
        pos[r] = a * K + b
    return sel, pos


def _split3(x):
    h1 = x.astype(jnp.bfloat16)
    r1 = x - h1.astype(jnp.float32)
    h2 = r1.astype(jnp.bfloat16)
    h3 = (r1 - h2.astype(jnp.float32)).astype(jnp.bfloat16)
    return h1, h2, h3


def _pick_rows(sel, x):
    d = lambda t: jnp.dot(sel, t, preferred_element_type=jnp.float32)
    h1, h2, h3 = _split3(x)
    return (d(h1) + d(h2)) + d(h3)


def _topk_kernel(st_ref, sel_ref, pos_ref, idx_ref, gate_ref):
    K = PEER_TOPK
    ncol = st_ref.shape[2] // LANES
    key_order = lax.broadcasted_iota(jnp.int32, (PEER_N_KEYS, LANES), 0).astype(jnp.float32)
    sel_a = sel_ref[0]
    sel_b = sel_ref[1]
    pos = pos_ref[...]
    live = pos >= 0.0

    def per_col(j, _):
        col = pl.ds(pl.multiple_of(j * LANES, LANES), LANES)
        picks = []
        for h in range(PEER_HEADS):
            s1, i1 = _top16(st_ref[2 * h, :, col], key_order)
            s2, i2 = _top16(st_ref[2 * h + 1, :, col], key_order)
            cand_s = jnp.where(live, _pick_rows(sel_a, s1) + _pick_rows(sel_b, s2), -jnp.inf)
            pick = lambda sel, t: jnp.dot(sel, t.astype(jnp.bfloat16), preferred_element_type=jnp.float32)
            cand_i = pick(sel_a, i1) * np.float32(PEER_N_KEYS) + pick(sel_b, i2)
            best_s, best_i = _top16(cand_s, pos, cand_i)
            e = jnp.exp(best_s - best_s[0:1])
            gate = e / jnp.sum(e, axis=0, keepdims=True)
            picks.append(best_i)
            gate_ref[h * K:(h + 1) * K, col] = gate
        idx_ref[col, :] = jnp.concatenate(picks, axis=0).T.astype(jnp.int32)
        return 0

    lax.fori_loop(0, ncol, per_col, 0)


PEER_TOKEN_TILE = 128
PEER_SELECTED = PEER_HEADS * PEER_TOPK
HALF_ROWS = D_MODEL // 2 // LANES
_HI_MASK = np.uint32(0xFFFF0000)


def _pack_table(tab):
    n, d = tab.shape
    b = lax.bitcast_convert_type(tab.astype(jnp.bfloat16), jnp.uint16).astype(jnp.uint32)
    w = (b[:, :d // 2] << 16) | b[:, d // 2:]
    return w.reshape(n, d // 2 // LANES, LANES)


def _unpack(w):
    hi = lax.bitcast_convert_type(w & _HI_MASK, jnp.float32)
    lo = lax.bitcast_convert_type(w << 16, jnp.float32)
    return hi, lo


def _peer_act_kernel(idx_ref, x_ref, gate_ref, tab_ref, c_ref, prod_scr, part_scr, acc_scr):
    TB = PEER_TOKEN_TILE
    NE = PEER_SELECTED
    lane = lax.broadcasted_iota(jnp.int32, (NE, TB), 1)
    acc_scr[...] = jnp.zeros_like(acc_scr)
    part_scr[...] = jnp.zeros_like(part_scr)

    def fold(t):
        act = jnp.sum(part_scr[...], axis=-1, keepdims=True)
        acc_scr[...] = jnp.where(lane == t, act, acc_scr[...])

    def per_token(t, _):
        fold(t - 1)
        xt = x_ref[t]
        xh = xt[:HALF_ROWS]
        xl = xt[HALF_ROWS:]
        for e in range(NE):
            hi, lo = _unpack(tab_ref[idx_ref[t, e]])
            prod_scr[e * HALF_ROWS:(e + 1) * HALF_ROWS, :] = hi * xh + lo * xl
        part = prod_scr[pl.ds(0, NE, stride=HALF_ROWS), :]
        for s in range(1, HALF_ROWS):
            part = part + prod_scr[pl.ds(s, NE, stride=HALF_ROWS), :]
        part_scr[...] = part
        return 0

    lax.fori_loop(0, TB, per_token, 0)
    fold(TB - 1)
    a = acc_scr[...]
    gelu = 0.5 * a * (1.0 + lax.erf(a * np.float32(1.0 / np.sqrt(2.0))))
    c_ref[...] = (gate_ref[...] * gelu).T


def _peer_act(idx, h2_3d, gate_t, u_packed, start, count):
    T, NE = idx.shape
    TB = PEER_TOKEN_TILE
    nrow = D_MODEL // LANES
    first = start // TB
    return pl.pallas_call(
        _peer_act_kernel,
        grid=(count // TB,),
        in_specs=[
            pl.BlockSpec((TB, NE), lambda i: (first + i, 0), memory_space=pltpu.SMEM),
            pl.BlockSpec((TB, nrow, LANES), lambda i: (first + i, 0, 0)),
            pl.BlockSpec((NE, TB), lambda i: (0, first + i)),
            pl.BlockSpec(u_packed.shape, lambda i: (0, 0, 0), pipeline_mode=pl.Buffered(1)),
        ],
        out_specs=pl.BlockSpec((TB, NE), lambda i: (i, 0)),
        out_shape=jax.ShapeDtypeStruct((count, NE), jnp.float32),
        scratch_shapes=[pltpu.VMEM((NE * HALF_ROWS, LANES), jnp.float32), pltpu.VMEM((NE, LANES), jnp.float32),
                        pltpu.VMEM((NE, TB), jnp.float32)],
        compiler_params=_cparams(("arbitrary",)),
        name="peer_act",
    )(idx, h2_3d, gate_t, u_packed)


GATHER_STRIDE = PEER_SELECTED + SUBLANES


def _peer_out_kernel_inplace(idx_ref, c_ref, x1_ref, g_ref, tab_ref, buf_ref, o_ref, gat_a, gat_b):
    del buf_ref
    _peer_out_kernel(idx_ref, c_ref, x1_ref, g_ref, tab_ref, o_ref, gat_a, gat_b)


def _peer_out_kernel(idx_ref, c_ref, x1_ref, g_ref, tab_ref, o_ref, gat_a, gat_b):
    TB = PEER_TOKEN_TILE
    NE = PEER_SELECTED
    GS = GATHER_STRIDE
    sub = lax.broadcasted_iota(jnp.int32, (SUBLANES, NE), 0)

    def gather(buf, t):
        for e in range(NE):
            buf[pl.ds(e, HALF_ROWS, stride=GS), :] = tab_ref[idx_ref[t, e]]

    def tile(buf):
        his, los = [], []
        for s in range(HALF_ROWS):
            hi, lo = _unpack(buf[s * GS:s * GS + NE, :])
            his.append(hi.astype(jnp.bfloat16))
            los.append(lo.astype(jnp.bfloat16))
        return jnp.concatenate(his + los, axis=1)

    def per_group(gi, _):
        base = pl.multiple_of(gi * SUBLANES, SUBLANES)
        c8 = c_ref[pl.ds(base, SUBLANES), :]
        y8 = jnp.zeros((SUBLANES, D_MODEL), jnp.float32)
        for tt in range(SUBLANES):
            buf = gat_a if tt % 2 == 0 else gat_b
            gather(buf, base + tt)
            ch, cl = _split(jnp.where(sub == tt, c8, 0.0))
            out = jnp.dot(jnp.concatenate([ch, cl], axis=0), tile(buf), preferred_element_type=jnp.float32)
            y8 = y8 + (out[:SUBLANES] + out[SUBLANES:])
        z = x1_ref[pl.ds(base, SUBLANES), :] + y8
        ms = jnp.mean(z * z, axis=-1, keepdims=True)
        o_ref[pl.ds(base, SUBLANES), :] = z * lax.rsqrt(ms + RMS_EPS) * g_ref[...]
        return 0

    lax.fori_loop(0, TB // SUBLANES, per_group, 0)


def _peer_out(idx, c, x1, lnf_g, v_packed, n_tokens, out_buf, out_row, total_rows):
    T, NE = idx.shape
    TB = PEER_TOKEN_TILE
    gat = pltpu.VMEM((HALF_ROWS * GATHER_STRIDE, LANES), jnp.uint32)
    first = out_row // TB
    in_specs = [
        pl.BlockSpec((TB, NE), lambda i: (i, 0), memory_space=pltpu.SMEM),
        pl.BlockSpec((TB, NE), lambda i: (i, 0)),
        pl.BlockSpec((TB, D_MODEL), lambda i: (i, 0)),
        pl.BlockSpec((1, D_MODEL), lambda i: (0, 0)),
        pl.BlockSpec(v_packed.shape, lambda i: (0, 0, 0), pipeline_mode=pl.Buffered(1)),
    ]
    args = [idx, c, x1, lnf_g.reshape(1, D_MODEL), v_packed]
    body = _peer_out_kernel
    aliases = {}
    if out_buf is not None:
        in_specs.append(pl.BlockSpec(memory_space=pl.ANY))
        args.append(out_buf)
        aliases = {len(args) - 1: 0}
        body = _peer_out_kernel_inplace
    return pl.pallas_call(
        body,
        grid=(n_tokens // TB,),
        in_specs=in_specs,
        out_specs=pl.BlockSpec((TB, D_MODEL), lambda i: (first + i, 0)),
        out_shape=jax.ShapeDtypeStruct((total_rows, D_MODEL), jnp.float32),
        scratch_shapes=[gat, gat],
        input_output_aliases=aliases,
        compiler_params=_cparams(("arbitrary",)),
        name="peer_out",
    )(*args)


SC_LANES = 16
SC_WORKERS = 32
SC_GATHER_ROWS = 32
SC_TOKEN_BLOCK = 4
SC_TOKEN_ALIGN = 256


SC_GROUPS = ((8, 16), (8, 16), (8, 16), (8, 10))


def _sc_tokens(tokens, share):
    return (tokens * share // 16) // SC_TOKEN_ALIGN * SC_TOKEN_ALIGN


def _peer_out_sc(idx, c, tab):
    n, NE = idx.shape
    D = 2 * tab.shape[1] * tab.shape[2]
    L = SC_LANES
    R = SC_GATHER_ROWS
    nrow = D // LANES
    TOK = SC_TOKEN_BLOCK
    nchunk = NE // R
    per_w = n // SC_WORKERS
    mesh = plsc.VectorSubcoreMesh(core_axis_name="c", subcore_axis_name="s")

    @functools.partial(
        pl.kernel, mesh=mesh, compiler_params=pltpu.CompilerParams(needs_layout_passes=False),
        out_type=jax.ShapeDtypeStruct((n, nrow, LANES), jnp.float32),
        scratch_types=[
            pltpu.VMEM((TOK, NE), jnp.int32),
            pltpu.VMEM((TOK, NE), jnp.float32),
            pltpu.VMEM((2, R, HALF_ROWS, LANES), jnp.uint32),
            pltpu.VMEM((2, nrow, LANES), jnp.float32),
            pltpu.SemaphoreType.DMA((2,)),
            pltpu.SemaphoreType.DMA((2,)),
        ],
    )
    def sc_kernel(idx_hbm, c_hbm, tab_hbm, out_hbm, idx_v, c_v, rows_v, y_v, sems, osems):
        wid = lax.axis_index("s") * 2 + lax.axis_index("c")

        def gather(tt, ci, b):
            return pltpu.make_async_copy(tab_hbm.at[idx_v.at[tt, pl.ds(ci * R, R)]], rows_v.at[b], sems.at[b])

        def put(t, yb):
            return pltpu.make_async_copy(y_v.at[yb], out_hbm.at[t], osems.at[yb])

        @pl.loop(0, per_w // TOK)
        def _(blk):
            t0 = wid * per_w + blk * TOK
            pltpu.sync_copy(idx_hbm.at[pl.ds(t0, TOK)], idx_v)
            pltpu.sync_copy(c_hbm.at[pl.ds(t0, TOK)], c_v)
            gather(0, 0, 0).start()
            for tt in range(TOK):
                yb = tt % 2
                if tt >= 2:
                    put(t0 + tt - 2, yb).wait()
                for ci in range(nchunk):
                    b = (tt * nchunk + ci) % 2
                    if ci + 1 < nchunk:
                        gather(tt, ci + 1, 1 - b).start()
                    elif tt + 1 < TOK:
                        gather(tt + 1, 0, 1 - b).start()
                    gather(tt, ci, b).wait()
                    for g in range(HALF_ROWS):
                        cols = [pl.ds(j * L, L) for j in range(LANES // L)]
                        if ci == 0:
                            accs = tuple(jnp.zeros((L,), jnp.float32) for _ in range(2 * len(cols)))
                        else:
                            accs = (tuple(y_v[yb, g, cs] for cs in cols)
                                    + tuple(y_v[yb, HALF_ROWS + g, cs] for cs in cols))

                        def row_body(e, accs, b=b, ci=ci, cols=cols, g=g, tt=tt):
                            lane0 = lax.iota(jnp.int32, L) * 0
                            ce = plsc.load_gather(c_v, [lane0 + tt, lane0 + (ci * R + e)])
                            his, los = [], []
                            for j, cs in enumerate(cols):
                                hi, lo = _unpack(rows_v[b, e, g, cs])
                                his.append(accs[j] + ce * hi)
                                los.append(accs[len(cols) + j] + ce * lo)
                            return tuple(his + los)

                        accs = lax.fori_loop(0, R, row_body, accs)
                        for j, cs in enumerate(cols):
                            y_v[yb, g, cs] = accs[j]
                            y_v[yb, HALF_ROWS + g, cs] = accs[len(cols) + j]
                put(t0 + tt, yb).start()
            put(t0 + TOK - 2, 0).wait()
            put(t0 + TOK - 1, 1).wait()

    return sc_kernel(idx, c, tab).reshape(n, D)


def _tail_norm_kernel(buf_ref, x1_ref, y_ref, g_ref, o_ref):
    del buf_ref
    z = x1_ref[...] + y_ref[...]
    ms = jnp.mean(z * z, axis=-1, keepdims=True)
    o_ref[...] = z * lax.rsqrt(ms + RMS_EPS) * g_ref[...]


def _tail_norm(out_buf, x1, y_tail, lnf_g, tm, start, out_row):
    T, D = x1.shape
    n = y_tail.shape[0]
    first_in = start // tm
    first_out = out_row // tm
    return pl.pallas_call(
        _tail_norm_kernel,
        grid=(n // tm,),
        in_specs=[
            pl.BlockSpec(memory_space=pl.ANY),
            pl.BlockSpec((tm, D), lambda i: (first_in + i, 0)),
            pl.BlockSpec((tm, D), lambda i: (i, 0)),
            pl.BlockSpec((1, D), lambda i: (0, 0)),
        ],
        out_specs=pl.BlockSpec((tm, D), lambda i: (first_out + i, 0)),
        out_shape=jax.ShapeDtypeStruct(out_buf.shape, jnp.float32),
        input_output_aliases={0: 0},
        compiler_params=_cparams(("parallel",)),
        name="peer_tail_norm",
    )(out_buf, x1, y_tail, lnf_g.reshape(1, D))


def _block(x2, first_row, B, S, out_buf, sc_share, ln1_g, w_in, b_attn, mu_shift, w0, w_up, a0, a_up, g_up,
           k_k, k_a, r_k, lnx_w, lnx_b, attn_sinks, attn_norm_g, w_out, ln2_g, peer_wq, peer_subkeys,
           u_packed, v_packed, lnf_g):
    T = B * S
    p_rkv, p_lora, p_attn = _inproj(x2, ln1_g, w_in, b_attn, 512, first_row, T)
    y_rwkv = _rwkv(p_rkv.reshape(B, S, -1), p_lora.reshape(B, S, -1), mu_shift, w0, w_up, a0,
                   a_up, g_up, k_k, k_a, r_k.reshape(-1), lnx_w, lnx_b)
    y_attn = _attention(p_attn.reshape(B, S, -1), attn_sinks, attn_norm_g)
    x1, h2_3d, idx, gate_t = _outproj(x2, y_rwkv.reshape(T, -1), y_attn.reshape(T, -1), w_out, ln2_g, peer_wq,
                                   peer_subkeys, 512, first_row)
    n_sc = _sc_tokens(T, sc_share)
    n_tc = T - n_sc
    tails = []
    if n_sc:
        c_sc = _peer_act(idx, h2_3d, gate_t, u_packed, n_tc, n_sc)
        c_sc, gate_t = lax.optimization_barrier((c_sc, gate_t))
        tails.append((x1, n_tc, _peer_out_sc(idx[n_tc:], c_sc, v_packed), first_row + n_tc))
    if n_tc:
        c = _peer_act(idx, h2_3d, gate_t, u_packed, 0, n_tc)
        out_buf = _peer_out(idx, c, x1, lnf_g, v_packed, n_tc, out_buf, first_row, x2.shape[0])
    return out_buf, tails


def kernel(x, ln1_g, w_in, b_attn, mu_shift, w0, w_up, a0, a_up, g_up, k_k, k_a, r_k, lnx_w, lnx_b, attn_sinks, attn_norm_g, w_out, ln2_g, peer_wq, peer_subkeys, peer_u, peer_v, lnf_g):
    B, S, D = x.shape
    groups = SC_GROUPS if B % 32 == 0 else ((32, 8),)
    x2 = x.reshape(B * S, D)
    u_packed = _pack_table(peer_u[0])
    v_packed = _pack_table(peer_v[0])
    out, tails, row = None, [], 0
    for frac, share in groups:
        rows = B * frac // 32
        out, tail = _block(x2, row * S, rows, S, out, share, ln1_g[0], w_in[0], b_attn[0], mu_shift[0],
                           w0[0], w_up[0], a0[0], a_up[0], g_up[0], k_k[0], k_a[0], r_k[0], lnx_w[0], lnx_b[0],
                           attn_sinks[0], attn_norm_g[0], w_out[0], ln2_g[0], peer_wq[0], peer_subkeys[0],
                           u_packed, v_packed, lnf_g)
        tails += tail
        row += rows
    for x1, start, y, out_row in tails:
        tm = 4 * SC_TOKEN_ALIGN
        while y.shape[0] % tm or start % tm or out_row % tm:
            tm //= 2
        out = _tail_norm(out, x1, y, lnf_g, tm, start, out_row)
    return out.reshape(B, S, D)
```
